```python
import jax, jax.numpy as jnp
from jax import lax
import numpy as np

D_MODEL = 1024
BATCH = 4
SEQ = 4096
DEPTH = 1
DEC_BATCH = 8
DEC_SEQ = 8192
PAST_LEN = 128

D_RWKV = 512
RWKV_HEAD = 64
RWKV_HEADS = D_RWKV // RWKV_HEAD
W_LORA = 32
A_LORA = 32
G_LORA = 96
GN_EPS = 64e-5
C_CONV = 512
CONV_W = 31
LN_EPS = 1e-5
N_BRANCHES = 2
N_EXPERTS = 16
D_FF_EXPERT = 1024
CAPACITY_FACTOR = 2
RMS_EPS = 1e-6

RWKV_COLS = 3 * D_RWKV + 2 * W_LORA + 2 * A_LORA + G_LORA
IN_COLS = RWKV_COLS + 2 * C_CONV + N_BRANCHES * D_MODEL

kernel_name = "hybrid_rwkv7_conformer_ec_moe_encoder"


def rms_norm(x, g):
    xf = x.astype(jnp.float32)
    y = xf * lax.rsqrt(jnp.mean(xf * xf, axis=-1, keepdims=True) + RMS_EPS)
    return (y * g.astype(jnp.float32)).astype(x.dtype)


def layer_norm(x, g, b):
    xf = x.astype(jnp.float32)
    mu = jnp.mean(xf, axis=-1, keepdims=True)
    var = jnp.mean(jnp.square(xf - mu), axis=-1, keepdims=True)
    y = (xf - mu) * lax.rsqrt(var + LN_EPS)
    return (y * g.astype(jnp.float32) + b.astype(jnp.float32)).astype(x.dtype)


def centred_shift(z, mu_prev, mu_next):
    zp = jnp.pad(z, ((0, 0), (1, 0), (0, 0)))[:, :-1]
    zn = jnp.pad(z, ((0, 0), (0, 1), (0, 0)))[:, 1:]
    return z + mu_prev * (zp - z) + mu_next * (zn - z)


def wkv7_scan(r, logw, k, v, a, b, reverse):
    B, T, H, N = r.shape
    decay = jnp.exp(-jnp.exp(logw.astype(jnp.float32)))
    xs = tuple(jnp.moveaxis(t.astype(jnp.float32), 1, 0) for t in (r, decay, k, v, a, b))

    def step(S, inp):
        r_t, w_t, k_t, v_t, a_t, b_t = inp
        sa = jnp.einsum('bhij,bhj->bhi', S, a_t)
        S = (S * w_t[:, :, None, :] + sa[..., :, None] * b_t[..., None, :]
             + v_t[..., :, None] * k_t[..., None, :])
        return S, jnp.einsum('bhij,bhj->bhi', S, r_t)

    S0 = jnp.zeros((B, H, N, N), jnp.float32)
    _, ys = lax.scan(step, S0, xs, reverse=reverse)
    return jnp.moveaxis(ys, 0, 1)


def rwkv7_branch(z, decay_w0, decay_up, iclr_a0, iclr_up, gate_up, k_k, k_a, r_k, lnx_w, lnx_b):
    B, T, _ = z.shape
    o = 3 * D_RWKV
    r = z[..., 0:D_RWKV]
    k = z[..., D_RWKV:2 * D_RWKV]
    v = z[..., 2 * D_RWKV:o]
    wd = z[..., o:o + 2 * W_LORA].reshape(B, T, 2, W_LORA)
    ad = z[..., o + 2 * W_LORA:o + 2 * W_LORA + 2 * A_LORA].reshape(B, T, 2, A_LORA)
    gd = z[..., o + 2 * W_LORA + 2 * A_LORA:]

    logw = -jax.nn.softplus(-(decay_w0 + jnp.einsum('btdr,drc->btdc', jnp.tanh(wd), decay_up))) - 0.5
    a = jax.nn.sigmoid(iclr_a0 + jnp.einsum('btdr,drc->btdc', ad, iclr_up))
    g = jax.nn.sigmoid(gd) @ gate_up

    heads = lambda t: t.reshape(B, T, RWKV_HEADS, RWKV_HEAD)
    kk = heads((k * k_k).astype(jnp.float32))
    kk = kk / jnp.maximum(jnp.linalg.norm(kk, axis=-1, keepdims=True), 1e-12)

    wkv = 0.0
    k_sum = 0.0
    for d, rev in ((0, False), (1, True)):
        a_d = a[:, :, d]
        k_d = k * (1.0 + (a_d - 1.0) * k_a)
        wkv = wkv + wkv7_scan(heads(r), heads(logw[:, :, d]), heads(k_d), heads(v),
                              -kk, kk * heads(a_d).astype(jnp.float32), rev)
        k_sum = k_sum + k_d

    mu = jnp.mean(wkv, axis=-1, keepdims=True)
    var = jnp.mean(jnp.square(wkv - mu), axis=-1, keepdims=True)
    y = ((wkv - mu) * lax.rsqrt(var + GN_EPS)).reshape(B, T, D_RWKV)
    y = y * lnx_w.astype(jnp.float32) + lnx_b.astype(jnp.float32)
    bonus = (jnp.sum(heads(r * k_sum * r_k.reshape(D_RWKV)), axis=-1, keepdims=True) * heads(v))
    y = (y + bonus.reshape(B, T, D_RWKV).astype(jnp.float32)) * g.astype(jnp.float32)
    return y.astype(z.dtype)


def conformer_conv_branch(u, dw_kernel, dw_bias, ln_w, ln_b):
    c = u[..., :C_CONV] * jax.nn.sigmoid(u[..., C_CONV:])
    c = lax.conv_general_dilated(
        c, dw_kernel[:, None, :].astype(c.dtype), window_strides=(1,),
        padding=[(CONV_W // 2, CONV_W // 2)],
        dimension_numbers=('NWC', 'WIO', 'NWC'),
        feature_group_count=C_CONV) + dw_bias
    return jax.nn.silu(layer_norm(c, ln_w, ln_b))


def expert_choice_ffn(h, router_w, router_b, w_gate, w_up, w_down):
    B, T, D = h.shape
    n_tok = B * T
    xf = h.reshape(n_tok, D)
    aff = jax.nn.softmax((xf @ router_w + router_b).astype(jnp.float32), axis=-1)
    cap = max(1, CAPACITY_FACTOR * n_tok // N_EXPERTS)
    gate, idx = lax.top_k(aff.T, cap)
    xe = xf[idx]
    hid = jax.nn.silu(jnp.einsum('ecd,edf->ecf', xe, w_gate)) * jnp.einsum('ecd,edf->ecf', xe, w_up)
    ye = jnp.einsum('ecf,efd->ecd', hid, w_down) * gate[..., None].astype(h.dtype)
    y = jnp.zeros((n_tok, D), h.dtype).at[idx.reshape(-1)].add(ye.reshape(-1, D).astype(h.dtype))
    return y.reshape(B, T, D)


def encoder_trunk(x, norm_mix, w_in, shift_mu_prev, shift_mu_next, decay_w0, decay_up, iclr_a0,
                  iclr_up, gate_up, k_k, k_a, r_k, lnx_w, lnx_b, w_o_rwkv, dw_kernel, dw_bias,
                  conv_ln_w, conv_ln_b, w_o_conv, w_out, norm_ffn, router_w, router_b,
                  exp_w_gate, exp_w_up, exp_w_down, norm_final):
    for l in range(DEPTH):
        h = rms_norm(x, norm_mix[l])
        p = h @ w_in[l]
        z = centred_shift(p[..., :RWKV_COLS], shift_mu_prev[l], shift_mu_next[l])
        u = p[..., RWKV_COLS:RWKV_COLS + 2 * C_CONV]
        gates = jax.nn.sigmoid(p[..., RWKV_COLS + 2 * C_CONV:])
        g_a, g_b = gates[..., :D_MODEL], gates[..., D_MODEL:]
        y_a = rwkv7_branch(z, decay_w0[l], decay_up[l], iclr_a0[l], iclr_up[l], gate_up[l],
                           k_k[l], k_a[l], r_k[l], lnx_w[l], lnx_b[l]) @ w_o_rwkv[l]
        y_b = conformer_conv_branch(u, dw_kernel[l], dw_bias[l], conv_ln_w[l], conv_ln_b[l]) @ w_o_conv[l]
        x = x + (g_a * y_a + g_b * y_b) @ w_out[l]
        x = x + expert_choice_ffn(rms_norm(x, norm_ffn[l]), router_w[l], router_b[l],
                                  exp_w_gate[l], exp_w_up[l], exp_w_down[l])
    return rms_norm(x, norm_final)


def setup_inputs(seed: int = 0) -> dict:
    key = jax.random.key(seed)
    ks = jax.random.split(key, 32)
    L, D, E, F = DEPTH, D_MODEL, N_EXPERTS, D_FF_EXPERT
    f32 = jnp.float32
    nrm = lambda k, shape, s: jax.random.normal(k, shape, f32) * s
    return {
        "x_prompt": nrm(ks[0], (BATCH, SEQ, D), 1.0),
        "x_sample": nrm(ks[1], (DEC_BATCH, DEC_SEQ, D), 1.0),
        "norm_mix": 1.0 + nrm(ks[2], (L, D), 0.05),
        "w_in": nrm(ks[3], (L, D, IN_COLS), D ** -0.5),
        "shift_mu_prev": jax.random.uniform(ks[4], (L, RWKV_COLS), f32, 0.0, 0.5),
        "shift_mu_next": jax.random.uniform(ks[5], (L, RWKV_COLS), f32, 0.0, 0.5),
        "decay_w0": jax.random.uniform(ks[6], (L, 2, D_RWKV), f32, -5.0, -1.0),
        "decay_up": nrm(ks[7], (L, 2, W_LORA, D_RWKV), 0.1),
        "iclr_a0": nrm(ks[8], (L, 2, D_RWKV), 0.1),
        "iclr_up": nrm(ks[9], (L, 2, A_LORA, D_RWKV), 0.1),
        "gate_up": nrm(ks[10], (L, G_LORA, D_RWKV), G_LORA ** -0.5),
        "k_k": 0.85 + nrm(ks[11], (L, D_RWKV), 0.05),
        "k_a": 1.0 + nrm(ks[12], (L, D_RWKV), 0.05),
        "r_k": nrm(ks[13], (L, RWKV_HEADS, RWKV_HEAD), 0.1),
        "lnx_w": 1.0 + nrm(ks[14], (L, D_RWKV), 0.05),
        "lnx_b": nrm(ks[15], (L, D_RWKV), 0.02),
        "w_o_rwkv": nrm(ks[16], (L, D_RWKV, D), D_RWKV ** -0.5),
        "dw_kernel": nrm(ks[17], (L, CONV_W, C_CONV), CONV_W ** -0.5),
        "dw_bias": nrm(ks[18], (L, C_CONV), 0.02),
        "conv_ln_w": 1.0 + nrm(ks[19], (L, C_CONV), 0.05),
        "conv_ln_b": nrm(ks[20], (L, C_CONV), 0.02),
        "w_o_conv": nrm(ks[21], (L, C_CONV, D), C_CONV ** -0.5),
        "w_out": nrm(ks[22], (L, D, D), D ** -0.5),
        "norm_ffn": 1.0 + nrm(ks[23], (L, D), 0.05),
        "router_w": nrm(ks[24], (L, D, E), D ** -0.5),
        "router_b": nrm(ks[25], (L, E), 0.01),
        "exp_w_gate": nrm(ks[26], (L, E, D, F), D ** -0.5),
        "exp_w_up": nrm(ks[27], (L, E, D, F), D ** -0.5),
        "exp_w_down": nrm(ks[28], (L, E, F, D), F ** -0.5),
        "norm_final": 1.0 + nrm(ks[29], (D,), 0.05),
    }


def reference(x_prompt, x_sample, norm_mix, w_in, shift_mu_prev, shift_mu_next, decay_w0, decay_up,
              iclr_a0, iclr_up, gate_up, k_k, k_a, r_k, lnx_w, lnx_b, w_o_rwkv, dw_kernel, dw_bias,
              conv_ln_w, conv_ln_b, w_o_conv, w_out, norm_ffn, router_w, router_b, exp_w_gate,
              exp_w_up, exp_w_down, norm_final):
    params = (norm_mix, w_in, shift_mu_prev, shift_mu_next, decay_w0, decay_up, iclr_a0, iclr_up,
              gate_up, k_k, k_a, r_k, lnx_w, lnx_b, w_o_rwkv, dw_kernel, dw_bias, conv_ln_w,
              conv_ln_b, w_o_conv, w_out, norm_ffn, router_w, router_b, exp_w_gate, exp_w_up,
              exp_w_down, norm_final)
    y_prompt = encoder_trunk(x_prompt, *params)
    y_sample = encoder_trunk(x_sample, *params)
    return (y_prompt, y_sample)
```

```python
import functools
import math

import jax
import jax.numpy as jnp
from jax import lax
from jax.experimental import pallas as pl
from jax.experimental.pallas import tpu as pltpu

f32 = jnp.float32
bf16 = jnp.bfloat16
i32 = jnp.int32

D_MODEL = 1024
D_RWKV = 512
HEAD = 64
W_LORA = 32
A_LORA = 32
G_LORA = 96
C_CONV = 512
CONV_W = 31
N_EXPERTS = 16
D_FF = 1024
CAPACITY_FACTOR = 2
GN_EPS = 64e-5
LN_EPS = 1e-5
RMS_EPS = 1e-6
LORA_COLS = 2 * W_LORA + 2 * A_LORA + G_LORA
LORA_PAD = 256
RKV_COLS = 3 * D_RWKV
PA_COLS = RKV_COLS + LORA_PAD
UP_COLS = 5 * D_RWKV

LANES_V7X = 128
MXU_DIM_V7X = 256
VMEM_LIMIT_V7X = 56 * 1024 * 1024

CHUNK = 64
GROUP = MXU_DIM_V7X // HEAD
GROUP_LANES = GROUP * HEAD
N_GROUPS = D_RWKV // GROUP_LANES

NN = (((1,), (0,)), ((), ()))
NT = (((1,), (1,)), ((), ()))
TN = (((0,), (0,)), ((), ()))


def _mm(a, b, dims=NN):
    return lax.dot_general(a.astype(bf16), b.astype(bf16), dims, preferred_element_type=f32)


def _mm_split(a, x, n):
    out = None
    rem = x
    for _ in range(n):
        hi = rem.astype(bf16)
        t = lax.dot_general(a, hi, NN, preferred_element_type=f32)
        out = t if out is None else out + t
        rem = rem - hi.astype(f32)
    return out


def _mm_split_lhs(x, a, n):
    out = None
    rem = x
    for _ in range(n):
        hi = rem.astype(bf16)
        t = lax.dot_general(hi, a, NN, preferred_element_type=f32)
        out = t if out is None else out + t
        rem = rem - hi.astype(f32)
    return out


def _sigmoid(x):
    return 1.0 / (1.0 + jnp.exp(-x))


def _rms(x, g):
    return x * lax.rsqrt(jnp.mean(x * x, axis=-1, keepdims=True) + RMS_EPS) * g


def _cparams(sem):
    return pltpu.CompilerParams(dimension_semantics=sem, vmem_limit_bytes=VMEM_LIMIT_V7X)


def _tile(n, pref):
    t = min(n, pref)
    assert n % t == 0, (n, pref)
    return t


def _inproj_kernel(x_ref, g_ref, wa_ref, wu_ref, p_ref, u_ref):
    h = _rms(x_ref[...], g_ref[...]).astype(bf16)
    p_ref[...] = jnp.dot(h, wa_ref[...], preferred_element_type=f32)
    u_ref[...] = jnp.dot(h, wu_ref[...], preferred_element_type=f32)


def _inproj(x2, g, wa, wu):
    n = x2.shape[0]
    tm = _tile(n, 512)
    return pl.pallas_call(
        _inproj_kernel,
        grid=(n // tm,),
        in_specs=[
            pl.BlockSpec((tm, D_MODEL), lambda i: (i, 0)),
            pl.BlockSpec((1, D_MODEL), lambda i: (0, 0)),
            pl.BlockSpec((D_MODEL, PA_COLS), lambda i: (0, 0)),
            pl.BlockSpec((D_MODEL, 2 * C_CONV), lambda i: (0, 0)),
        ],
        out_specs=[
            pl.BlockSpec((tm, PA_COLS), lambda i: (i, 0)),
            pl.BlockSpec((tm, 2 * C_CONV), lambda i: (i, 0)),
        ],
        out_shape=[
            jax.ShapeDtypeStruct((n, PA_COLS), f32),
            jax.ShapeDtypeStruct((n, 2 * C_CONV), f32),
        ],
        compiler_params=_cparams(("parallel",)),
        name="inproj",
    )(x2, g, wa, wu)


def _prep_kernel(pc_ref, pp_ref, pn_ref, mup_ref, mun_ref, wup_ref, w0_ref, a0_ref, kkw_ref, ka_ref,
                 rk_ref, hsum_ref, r_o, v_o, kk_o, ew0_o, ew1_o, k0_o, k1_o, b0_o, b1_o, g_o, bonus_o):
    tt = pc_ref.shape[1]
    i = pl.program_id(1)
    nt = pl.num_programs(1)
    row = lax.broadcasted_iota(i32, (tt, 1), 0)
    first = i == 0
    last = i == nt - 1

    def shifted(c0, c1):
        p = pc_ref[0, :, c0:c1]
        prev_row = jnp.where(first, 0.0, pp_ref[0, 7:8, c0:c1])
        next_row = jnp.where(last, 0.0, pn_ref[0, 0:1, c0:c1])
        zp = jnp.where(row == 0, prev_row, pltpu.roll(p, 1, axis=0))
        zn = jnp.where(row == tt - 1, next_row, pltpu.roll(p, tt - 1, axis=0))
        return p + mup_ref[:, c0:c1] * (zp - p) + mun_ref[:, c0:c1] * (zn - p)

    r = shifted(0, D_RWKV)
    k = shifted(D_RWKV, 2 * D_RWKV)
    v = shifted(2 * D_RWKV, 3 * D_RWKV)
    zl = shifted(RKV_COLS, PA_COLS)

    lane = lax.broadcasted_iota(i32, (1, LORA_PAD), 1)
    act = jnp.where(lane < 2 * W_LORA, jnp.tanh(zl),
                    jnp.where(lane < 2 * W_LORA + 2 * A_LORA, zl, _sigmoid(zl)))
    pre = jnp.dot(act.astype(bf16), wup_ref[...], preferred_element_type=f32)

    hsum = hsum_ref[...]
    kkr = k * kkw_ref[...]
    ss = _mm_split_lhs(kkr * kkr, hsum, 2)
    kk = kkr / jnp.maximum(jnp.sqrt(ss), 1e-12)
    ka = ka_ref[...]
    ksum = None
    outs = ((ew0_o, k0_o, b0_o), (ew1_o, k1_o, b1_o))
    for d in range(2):
        x = pre[:, d * D_RWKV:(d + 1) * D_RWKV] + w0_ref[d:d + 1, :]
        ew = math.exp(-0.5) * _sigmoid(x)
        alpha = _sigmoid(pre[:, (2 + d) * D_RWKV:(3 + d) * D_RWKV] + a0_ref[d:d + 1, :])
        kd = k * (1.0 + (alpha - 1.0) * ka)
        ew_o, k_o, b_o = outs[d]
        ew_o[0] = ew
        k_o[0] = kd
        b_o[0] = kk * alpha
        ksum = kd if ksum is None else ksum + kd
    bonus = _mm_split_lhs(r * ksum * rk_ref[...], hsum, 2) * v
    r_o[0] = r
    v_o[0] = v
    kk_o[0] = kk
    g_o[0] = pre[:, 4 * D_RWKV:5 * D_RWKV]
    bonus_o[0] = bonus


def _prep(p3, mup, mun, wup, w0, a0, kkw, ka, rk, hsum):
    b, t, _ = p3.shape
    tt = _tile(t, 256)
    nt = t // tt
    hb = tt // 8
    full = lambda s: pl.BlockSpec(s, lambda bi, ti: tuple(0 for _ in s))
    o_spec = pl.BlockSpec((1, tt, D_RWKV), lambda bi, ti: (bi, ti, 0))
    o_shape = jax.ShapeDtypeStruct((b, t, D_RWKV), f32)
    return pl.pallas_call(
        _prep_kernel,
        grid=(b, nt),
        in_specs=[
            pl.BlockSpec((1, tt, PA_COLS), lambda bi, ti: (bi, ti, 0)),
            pl.BlockSpec((1, 8, PA_COLS), lambda bi, ti: (bi, jnp.maximum(ti * hb - 1, 0), 0)),
            pl.BlockSpec((1, 8, PA_COLS), lambda bi, ti: (bi, jnp.minimum((ti + 1) * hb, t // 8 - 1), 0)),
            full((1, PA_COLS)), full((1, PA_COLS)), full((LORA_PAD, UP_COLS)),
            full((2, D_RWKV)), full((2, D_RWKV)), full((1, D_RWKV)), full((1, D_RWKV)), full((1, D_RWKV)),
            full((D_RWKV, D_RWKV)),
        ],
        out_specs=[o_spec] * 11,
        out_shape=[o_shape] * 11,
        compiler_params=_cparams(("parallel", "parallel")),
        name="rwkv_prep",
    )(p3, p3, p3, mup, mun, wup, w0, a0, kkw, ka, rk, hsum)


def _scan_consts(reverse):
    L = CHUNK
    t = lax.broadcasted_iota(i32, (L, GROUP * L), 0)
    s = lax.broadcasted_iota(i32, (L, GROUP * L), 1) & (L - 1)
    strict = (s > t) if reverse else (s < t)
    eye = s == t
    incl = strict | eye
    tt = lax.broadcasted_iota(i32, (L, L), 0)
    ss = lax.broadcasted_iota(i32, (L, L), 1)
    tri = jnp.where((ss >= tt) if reverse else (ss <= tt), 1.0, 0.0).astype(bf16)
    rr = lax.broadcasted_iota(i32, (GROUP_LANES, GROUP_LANES), 0) // HEAD
    cc = lax.broadcasted_iota(i32, (GROUP_LANES, GROUP_LANES), 1) // HEAD
    return strict, incl, eye, tri, rr == cc


def _bd(x, bdmask):
    xb = x.astype(bf16)
    return jnp.where(bdmask, jnp.concatenate([xb] * GROUP, axis=0), jnp.zeros((), bf16))


def _wkv_chunk(r, v, kk, ew, k, b, S, consts, reverse):
    L = CHUNK
    strict, incl, eye, tri, bdmask = consts
    ginc = -_mm_split(tri, ew, 3)
    gtot = ginc[0:1] if reverse else ginc[L - 1:L]
    gref = 0.5 * gtot
    egref = jnp.exp(gref)
    e1 = jnp.exp(ginc - gref)
    e2 = jnp.exp(gref - ginc)
    ea = jnp.exp(ginc + ew - gref)
    At = -kk * ea
    Rt = r * e1
    Kt = k * e2
    Bt = b * e2
    Be = Bt * egref
    AR = jnp.concatenate([At, Rt], axis=0)
    pb = _mm(AR, _bd(Bt, bdmask), NT)
    pk = _mm(AR, _bd(Kt, bdmask), NT)
    Mab = jnp.where(strict, pb[:L], 0.0)
    Arb = jnp.where(incl, pb[L:], 0.0)
    Mak = jnp.where(strict, pk[:L], 0.0)
    Ark = jnp.where(incl, pk[L:], 0.0)
    X = jnp.where(eye, 1.0, Mab)
    P = Mab
    for _ in range(5):
        P = _mm(P, _bd(P, bdmask))
        X = X + _mm(X, _bd(P, bdmask))
    vbd = _bd(v, bdmask)
    W1 = _mm(X, _bd(At * egref, bdmask))
    W2 = _mm(X, _bd(_mm(Mak, vbd), bdmask))
    Q = Rt * egref + _mm(Arb, _bd(W1, bdmask))
    Y0 = _mm(Arb, _bd(W2, bdmask)) + _mm(Ark, vbd)
    Pm = jnp.where(bdmask, _mm(W1, Be, TN), 0.0)
    Qm = jnp.where(bdmask, _mm(W2, Be, TN) + _mm(v, Kt * egref, TN), 0.0)
    Y = _mm(Q, S, NT) + Y0
    Sn = S * (egref * egref) + _mm(S, Pm) + Qm
    return Y, Sn


def _scan_kernel(rf, vf, kkf, ewf, kf, bf, rb, vb, kkb, ewb, kb, bb, yf_ref, yb_ref, s_ref):
    tb = rf.shape[1]
    nch = tb // CHUNK

    @pl.when(pl.program_id(1) == 0)
    def _():
        s_ref[...] = jnp.zeros_like(s_ref)

    consts_f = _scan_consts(False)
    consts_b = _scan_consts(True)

    def body(j, carry):
        for reverse, refs, y_ref, consts in ((False, (rf, vf, kkf, ewf, kf, bf), yf_ref, consts_f),
                                             (True, (rb, vb, kkb, ewb, kb, bb), yb_ref, consts_b)):
            cj = (nch - 1 - j) if reverse else j
            rows = pl.ds(pl.multiple_of(cj * CHUNK, CHUNK), CHUNK)
            for g in range(N_GROUPS):
                lanes = slice(g * GROUP_LANES, (g + 1) * GROUP_LANES)
                args = [ref[0, rows, lanes] for ref in refs]
                si = (2 if reverse else 0) + g
                y, sn = _wkv_chunk(*args, s_ref[si], consts, reverse)
                y_ref[0, rows, lanes] = y
                s_ref[si] = sn
        return carry

    lax.fori_loop(0, nch, body, 0)


def _scan(r, v, kk, ew0, ew1, k0, k1, b0, b1):
    b, t, _ = r.shape
    tb = _tile(t, 256)
    nb = t // tb
    fwd = pl.BlockSpec((1, tb, D_RWKV), lambda bi, ti: (bi, ti, 0))
    bwd = pl.BlockSpec((1, tb, D_RWKV), lambda bi, ti: (bi, nb - 1 - ti, 0))
    o_shape = jax.ShapeDtypeStruct((b, t, D_RWKV), f32)
    return pl.pallas_call(
        _scan_kernel,
        grid=(b, nb),
        in_specs=[fwd] * 6 + [bwd] * 6,
        out_specs=[fwd, bwd],
        out_shape=[o_shape, o_shape],
        scratch_shapes=[pltpu.VMEM((2 * N_GROUPS, GROUP_LANES, GROUP_LANES), f32)],
        compiler_params=_cparams(("parallel", "arbitrary")),
        name="wkv_scan",
    )(r, v, kk, ew0, k0, b0, r, v, kk, ew1, k1, b1)


CONV_HALO = 16


def _conv_kernel(uc_ref, up_ref, un_ref, kern_ref, bias_ref, lnw_ref, lnb_ref, o_ref, ext_ref):
    tt = uc_ref.shape[1]
    i = pl.program_id(1)
    nt = pl.num_programs(1)

    def glu(u):
        return u[:, :C_CONV] * _sigmoid(u[:, C_CONV:])

    ext_ref[0:CONV_HALO, :] = jnp.where(i == 0, 0.0, glu(up_ref[0]))
    ext_ref[CONV_HALO:CONV_HALO + tt, :] = glu(uc_ref[0])
    ext_ref[CONV_HALO + tt:, :] = jnp.where(i == nt - 1, 0.0, glu(un_ref[0]))
    acc = jnp.zeros((tt, C_CONV), f32) + bias_ref[...]
    half = CONV_W // 2
    for j in range(CONV_W):
        off = CONV_HALO - half + j
        acc = acc + kern_ref[j:j + 1, :] * ext_ref[off:off + tt, :]
    mu = jnp.mean(acc, axis=-1, keepdims=True)
    xc = acc - mu
    var = jnp.mean(xc * xc, axis=-1, keepdims=True)
    y = xc * lax.rsqrt(var + LN_EPS) * lnw_ref[...] + lnb_ref[...]
    o_ref[0] = y * _sigmoid(y)


def _conv(u3, kern, bias, lnw, lnb):
    b, t, _ = u3.shape
    tt = _tile(t, 512)
    hb = tt // CONV_HALO
    full = lambda s: pl.BlockSpec(s, lambda bi, ti: tuple(0 for _ in s))
    return pl.pallas_call(
        _conv_kernel,
        grid=(b, t // tt),
        in_specs=[
            pl.BlockSpec((1, tt, 2 * C_CONV), lambda bi, ti: (bi, ti, 0)),
            pl.BlockSpec((1, CONV_HALO, 2 * C_CONV), lambda bi, ti: (bi, jnp.maximum(ti * hb - 1, 0), 0)),
            pl.BlockSpec((1, CONV_HALO, 2 * C_CONV),
                         lambda bi, ti: (bi, jnp.minimum((ti + 1) * hb, t // CONV_HALO - 1), 0)),
            full((CONV_W, C_CONV)), full((1, C_CONV)), full((1, C_CONV)), full((1, C_CONV)),
        ],
        out_specs=pl.BlockSpec((1, tt, C_CONV), lambda bi, ti: (bi, ti, 0)),
        out_shape=jax.ShapeDtypeStruct((b, t, C_CONV), f32),
        scratch_shapes=[pltpu.VMEM((tt + 2 * CONV_HALO, C_CONV), f32)],
        compiler_params=_cparams(("parallel", "parallel")),
        name="conv_branch",
    )(u3, u3, u3, kern, bias, lnw, lnb)


def _merge_kernel(x_ref, yf_ref, yb_ref, g_ref, bonus_ref, cb_ref, nmix_ref, wg_ref, lnxw_ref, lnxb_ref,
                  havg_ref, wor_ref, woc_ref, wout_ref, nffn_ref, rwt_ref, rb_ref,
                  x1_ref, h2_ref, aff_ref):
    x = x_ref[...]
    h = _rms(x, nmix_ref[...]).astype(bf16)
    gates = _sigmoid(jnp.dot(h, wg_ref[...], preferred_element_type=f32))
    wkv = yf_ref[...] + yb_ref[...]
    havg = havg_ref[...]
    mu = _mm_split_lhs(wkv, havg, 2)
    xc = wkv - mu
    var = _mm_split_lhs(xc * xc, havg, 2)
    y = (xc * lax.rsqrt(var + GN_EPS) * lnxw_ref[...] + lnxb_ref[...] + bonus_ref[...]) * g_ref[...]
    y_a = jnp.dot(y.astype(bf16), wor_ref[...], preferred_element_type=f32)
    y_b = jnp.dot(cb_ref[...].astype(bf16), woc_ref[...], preferred_element_type=f32)
    m = gates[:, :D_MODEL] * y_a + gates[:, D_MODEL:] * y_b
    x1 = x + jnp.dot(m.astype(bf16), wout_ref[...], preferred_element_type=f32)
    x1_ref[...] = x1
    h2 = _rms(x1, nffn_ref[...])
    h2_ref[...] = h2
    logits = lax.dot_general(rwt_ref[...], h2.astype(bf16), NT, preferred_element_type=f32) + rb_ref[...]
    mx = jnp.max(logits, axis=0, keepdims=True)
    e = jnp.exp(logits - mx)
    aff_ref[...] = e / jnp.sum(e, axis=0, keepdims=True)


def _merge(x2, yf, yb, g, bonus, cb, nmix, wg, lnxw, lnxb, havg, wor, woc, wout, nffn, rwt, rb):
    n = x2.shape[0]
    tm = _tile(n, 512)
    row = lambda c: pl.BlockSpec((tm, c), lambda i: (i, 0))
    full = lambda s: pl.BlockSpec(s, lambda i: tuple(0 for _ in s))
    return pl.pallas_call(
        _merge_kernel,
        grid=(n // tm,),
        in_specs=[
            row(D_MODEL), row(D_RWKV), row(D_RWKV), row(D_RWKV), row(D_RWKV), row(C_CONV),
            full((1, D_MODEL)), full((D_MODEL, 2 * D_MODEL)), full((1, D_RWKV)), full((1, D_RWKV)),
            full((D_RWKV, D_RWKV)), full((D_RWKV, D_MODEL)), full((C_CONV, D_MODEL)),
            full((D_MODEL, D_MODEL)), full((1, D_MODEL)), full((N_EXPERTS, D_MODEL)), full((N_EXPERTS, 1)),
        ],
        out_specs=[row(D_MODEL), row(D_MODEL), pl.BlockSpec((N_EXPERTS, tm), lambda i: (0, i))],
        out_shape=[
            jax.ShapeDtypeStruct((n, D_MODEL), f32),
            jax.ShapeDtypeStruct((n, D_MODEL), f32),
            jax.ShapeDtypeStruct((N_EXPERTS, n), f32),
        ],
        compiler_params=_cparams(("parallel",)),
        name="merge_router",
    )(x2, yf, yb, g, bonus, cb, nmix, wg, lnxw, lnxb, havg, wor, woc, wout, nffn, rwt, rb)


def _thr_kernel(aff_ref, thr_ref, *, cap):
    def count_ge(mid):
        bits = pltpu.bitcast(aff_ref[...], i32)
        return jnp.sum(jnp.where(bits >= mid, 1.0, 0.0), axis=1, keepdims=True)

    def body(_, c):
        lo, hi = c
        mid = lo + ((hi - lo + 1) >> 1)
        ok = count_ge(mid) >= cap
        return jnp.where(ok, mid, lo), jnp.where(ok, hi, mid - 1)

    lo0 = jnp.zeros((N_EXPERTS, 1), i32)
    hi0 = jnp.full((N_EXPERTS, 1), 0x3F800000, i32)
    lo, _ = lax.fori_loop(0, 31, body, (lo0, hi0))
    thr_ref[...] = jnp.broadcast_to(lo, thr_ref.shape)


def _threshold(aff_t, cap):
    e, n = aff_t.shape
    return pl.pallas_call(
        functools.partial(_thr_kernel, cap=float(cap)),
        in_specs=[pl.BlockSpec((e, n), lambda: (0, 0))],
        out_specs=pl.BlockSpec((e, LANES_V7X), lambda: (0, 0)),
        out_shape=jax.ShapeDtypeStruct((e, LANES_V7X), i32),
        compiler_params=pltpu.CompilerParams(vmem_limit_bytes=VMEM_LIMIT_V7X),
        name="route_threshold",
    )(aff_t)


def _select(aff_t, cap):
    thr = _threshold(aff_t, cap)[:, :1]
    bits = lax.bitcast_convert_type(aff_t, i32)
    gt = bits > thr
    eq = bits == thr
    need = cap - jnp.sum(gt, axis=1, keepdims=True)
    sel = gt | (eq & (jnp.cumsum(eq, axis=1) <= need))
    cum = jnp.cumsum(sel, axis=1)
    slots = jnp.arange(1, cap + 1, dtype=cum.dtype)
    idx = jax.vmap(lambda c: jnp.searchsorted(c, slots, side="left"))(cum).astype(i32)
    gate = jnp.take_along_axis(aff_t, idx, axis=1)
    return idx, gate


def _expert_kernel(idx_ref, gate_ref, h2_hbm, acc_in_hbm, wg_ref, wu_ref, wd_ref, acc_hbm, xbuf, abuf, sem):
    del acc_in_hbm
    tc = xbuf.shape[0]

    def x_copy(i):
        return pltpu.make_async_copy(h2_hbm.at[pl.ds(idx_ref[0, 0, i], 1)], xbuf.at[pl.ds(i, 1)], sem.at[0])

    def a_in_copy(i):
        return pltpu.make_async_copy(acc_hbm.at[pl.ds(idx_ref[0, 0, i], 1)], abuf.at[pl.ds(i, 1)], sem.at[1])

    def a_out_copy(i):
        return pltpu.make_async_copy(abuf.at[pl.ds(i, 1)], acc_hbm.at[pl.ds(idx_ref[0, 0, i], 1)], sem.at[2])

    def start_in(i, c):
        x_copy(i).start()
        a_in_copy(i).start()
        return c

    def wait_in(i, c):
        x_copy(i).wait()
        a_in_copy(i).wait()
        return c

    lax.fori_loop(0, tc, start_in, 0)
    lax.fori_loop(0, tc, wait_in, 0)
    xe = xbuf[...].astype(bf16)
    hg = jnp.dot(xe, wg_ref[0], preferred_element_type=f32)
    hu = jnp.dot(xe, wu_ref[0], preferred_element_type=f32)
    hid = (hg * _sigmoid(hg) * hu).astype(bf16)
    ye = jnp.dot(hid, wd_ref[0], preferred_element_type=f32) * gate_ref[0]
    abuf[...] = abuf[...] + ye

    def start_out(i, c):
        a_out_copy(i).start()
        return c

    def wait_out(i, c):
        a_out_copy(i).wait()
        return c

    lax.fori_loop(0, tc, start_out, 0)
    lax.fori_loop(0, tc, wait_out, 0)


def _experts(idx, gate, h2, x1, wg, wu, wd):
    e, cap = idx.shape
    n = h2.shape[0]
    tc = _tile(cap, 256)
    nt = cap // tc
    wspec = pl.BlockSpec((1, D_MODEL, D_FF), lambda ei, ji: (ei, 0, 0))
    return pl.pallas_call(
        _expert_kernel,
        grid=(e, nt),
        in_specs=[
            pl.BlockSpec((1, 1, tc), lambda ei, ji: (ei * nt + ji, 0, 0), memory_space=pltpu.SMEM),
            pl.BlockSpec((1, tc, 1), lambda ei, ji: (ei * nt + ji, 0, 0)),
            pl.BlockSpec(memory_space=pl.ANY),
            pl.BlockSpec(memory_space=pl.ANY),
            wspec, wspec,
            pl.BlockSpec((1, D_FF, D_MODEL), lambda ei, ji: (ei, 0, 0)),
        ],
        out_specs=pl.BlockSpec(memory_space=pl.ANY),
        out_shape=jax.ShapeDtypeStruct((n, D_MODEL), f32),
        input_output_aliases={3: 0},
        scratch_shapes=[pltpu.VMEM((tc, D_MODEL), f32), pltpu.VMEM((tc, D_MODEL), f32),
                        pltpu.SemaphoreType.DMA((3,))],
        compiler_params=_cparams(("arbitrary", "arbitrary")),
        name="expert_ffn",
    )(idx.reshape(e * nt, 1, tc), gate.reshape(e * nt, tc, 1), h2, x1, wg, wu, wd)


def _final_kernel(x_ref, g_ref, o_ref):
    o_ref[...] = _rms(x_ref[...], g_ref[...])


def _final(x2, g):
    n = x2.shape[0]
    tm = _tile(n, 1024)
    return pl.pallas_call(
        _final_kernel,
        grid=(n // tm,),
        in_specs=[pl.BlockSpec((tm, D_MODEL), lambda i: (i, 0)), pl.BlockSpec((1, D_MODEL), lambda i: (0, 0))],
        out_specs=pl.BlockSpec((tm, D_MODEL), lambda i: (i, 0)),
        out_shape=jax.ShapeDtypeStruct((n, D_MODEL), f32),
        compiler_params=_cparams(("parallel",)),
        name="final_norm",
    )(x2, g)


def _prepare_params(norm_mix, w_in, shift_mu_prev, shift_mu_next, decay_w0, decay_up, iclr_a0, iclr_up,
                    gate_up, k_k, k_a, r_k, lnx_w, lnx_b, w_o_rwkv, dw_kernel, dw_bias, conv_ln_w, conv_ln_b,
                    w_o_conv, w_out, norm_ffn, router_w, router_b, exp_w_gate, exp_w_up, exp_w_down, norm_final):
    l = 0
    rwkv_cols = RKV_COLS + LORA_COLS
    w = w_in[l]
    pad = jnp.zeros((D_MODEL, LORA_PAD - LORA_COLS), f32)
    wa = jnp.concatenate([w[:, :rwkv_cols], pad], axis=1).astype(bf16)
    wu = w[:, rwkv_cols:rwkv_cols + 2 * C_CONV].astype(bf16)
    wgate = w[:, rwkv_cols + 2 * C_CONV:].astype(bf16)
    padv = jnp.zeros((LORA_PAD - LORA_COLS,), f32)
    mup = jnp.concatenate([shift_mu_prev[l], padv])[None, :]
    mun = jnp.concatenate([shift_mu_next[l], padv])[None, :]
    wup = jnp.zeros((LORA_PAD, UP_COLS), f32)
    wup = wup.at[0:W_LORA, 0:D_RWKV].set(decay_up[l, 0])
    wup = wup.at[W_LORA:2 * W_LORA, D_RWKV:2 * D_RWKV].set(decay_up[l, 1])
    o = 2 * W_LORA
    wup = wup.at[o:o + A_LORA, 2 * D_RWKV:3 * D_RWKV].set(iclr_up[l, 0])
    wup = wup.at[o + A_LORA:o + 2 * A_LORA, 3 * D_RWKV:4 * D_RWKV].set(iclr_up[l, 1])
    o = 2 * W_LORA + 2 * A_LORA
    wup = wup.at[o:o + G_LORA, 4 * D_RWKV:5 * D_RWKV].set(gate_up[l])
    head = jnp.arange(D_RWKV) // HEAD
    same = (head[:, None] == head[None, :])
    return dict(
        norm_mix=norm_mix[l][None, :], wa=wa, wu=wu, wgate=wgate, mup=mup, mun=mun, wup=wup.astype(bf16),
        w0=decay_w0[l], a0=iclr_a0[l], kkw=k_k[l][None, :], ka=k_a[l][None, :], rk=r_k[l].reshape(1, D_RWKV),
        hsum=same.astype(bf16), havg=(same.astype(f32) / HEAD).astype(bf16),
        lnxw=lnx_w[l][None, :], lnxb=lnx_b[l][None, :], wor=w_o_rwkv[l].astype(bf16),
        kern=dw_kernel[l], cbias=dw_bias[l][None, :], clnw=conv_ln_w[l][None, :], clnb=conv_ln_b[l][None, :],
        woc=w_o_conv[l].astype(bf16), wout=w_out[l].astype(bf16), nffn=norm_ffn[l][None, :],
        rwt=router_w[l].T.astype(bf16), rb=router_b[l][:, None],
        ewg=exp_w_gate[l].astype(bf16), ewu=exp_w_up[l].astype(bf16), ewd=exp_w_down[l].astype(bf16),
        nfinal=norm_final[None, :],
    )


def _trunk(x, P):
    b, t, d = x.shape
    n = b * t
    x2 = x.reshape(n, d)
    p, u = _inproj(x2, P["norm_mix"], P["wa"], P["wu"])
    r, v, kk, ew0, ew1, k0, k1, b0, b1, g, bonus = _prep(
        p.reshape(b, t, PA_COLS), P["mup"], P["mun"], P["wup"], P["w0"], P["a0"], P["kkw"], P["ka"], P["rk"],
        P["hsum"])
    yf, yb = _scan(r, v, kk, ew0, ew1, k0, k1, b0, b1)
    cb = _conv(u.reshape(b, t, 2 * C_CONV), P["kern"], P["cbias"], P["clnw"], P["clnb"])
    flat = lambda a: a.reshape(n, a.shape[-1])
    x1, h2, aff_t = _merge(x2, flat(yf), flat(yb), flat(g), flat(bonus), flat(cb), P["norm_mix"], P["wgate"],
                           P["lnxw"], P["lnxb"], P["havg"], P["wor"], P["woc"], P["wout"], P["nffn"],
                           P["rwt"], P["rb"])
    cap = max(1, CAPACITY_FACTOR * n // N_EXPERTS)
    idx, gate = _select(aff_t, cap)
    x2o = _experts(idx, gate, h2, x1, P["ewg"], P["ewu"], P["ewd"])
    return _final(x2o, P["nfinal"]).reshape(b, t, d)


def kernel(x_prompt, x_sample, norm_mix, w_in, shift_mu_prev, shift_mu_next, decay_w0, decay_up, iclr_a0, iclr_up, gate_up, k_k, k_a, r_k, lnx_w, lnx_b, w_o_rwkv, dw_kernel, dw_bias, conv_ln_w, conv_ln_b, w_o_conv, w_out, norm_ffn, router_w, router_b, exp_w_gate, exp_w_up, exp_w_down, norm_final):
    P = _prepare_params(norm_mix, w_in, shift_mu_prev, shift_mu_next, decay_w0, decay_up, iclr_a0, iclr_up,
                        gate_up, k_k, k_a, r_k, lnx_w, lnx_b, w_o_rwkv, dw_kernel, dw_bias, conv_ln_w,
                        conv_ln_b, w_o_conv, w_out, norm_ffn, router_w, router_b, exp_w_gate, exp_w_up,
                        exp_w_down, norm_final)
    return (_trunk(x_prompt, P), _trunk(x_sample, P))
```

```python
import functools
import math

import jax
import jax.numpy as jnp
from jax import lax
from jax.experimental import pallas as pl
from jax.experimental.pallas import tpu as pltpu

f32 = jnp.float32
bf16 = jnp.bfloat16
i32 = jnp.int32

D_MODEL = 1024
D_RWKV = 512
HEAD = 64
W_LORA = 32
A_LORA = 32
G_LORA = 96
C_CONV = 512
CONV_W = 31
N_EXPERTS = 16
D_FF = 1024
CAPACITY_FACTOR = 2
GN_EPS = 64e-5
LN_EPS = 1e-5
RMS_EPS = 1e-6
LORA_COLS = 2 * W_LORA + 2 * A_LORA + G_LORA
LORA_PAD = 256
RKV_COLS = 3 * D_RWKV
PA_COLS = RKV_COLS + LORA_PAD
UP_COLS = 5 * D_RWKV

LANES_V7X = 128
MXU_DIM_V7X = 256
VMEM_LIMIT_V7X = 56 * 1024 * 1024

CHUNK = 64
SCAN_UNROLL = 2
SCAN_BLOCK = 512
GROUP = MXU_DIM_V7X // HEAD
GROUP_LANES = GROUP * HEAD
N_GROUPS = D_RWKV // GROUP_LANES

NN = (((1,), (0,)), ((), ()))
NT = (((1,), (1,)), ((), ()))
TN = (((0,), (0,)), ((), ()))


def _mm(a, b, dims=NN):
    return lax.dot_general(a.astype(bf16), b.astype(bf16), dims, preferred_element_type=f32)


def _mm_split(a, x, n):
    out = None
    rem = x
    for _ in range(n):
        hi = rem.astype(bf16)
        t = lax.dot_general(a, hi, NN, preferred_element_type=f32)
        out = t if out is None else out + t
        rem = rem - hi.astype(f32)
    return out


def _mm_split_lhs(x, a, n):
    out = None
    rem = x
    for _ in range(n):
        hi = rem.astype(bf16)
        t = lax.dot_general(hi, a, NN, preferred_element_type=f32)
        out = t if out is None else out + t
        rem = rem - hi.astype(f32)
    return out


def _sigmoid(x):
    return 1.0 / (1.0 + jnp.exp(-x))


def _rms(x, g):
    return x * lax.rsqrt(jnp.mean(x * x, axis=-1, keepdims=True) + RMS_EPS) * g


def _cparams(sem):
    return pltpu.CompilerParams(dimension_semantics=sem, vmem_limit_bytes=VMEM_LIMIT_V7X)


def _tile(n, pref):
    t = min(n, pref)
    assert n % t == 0, (n, pref)
    return t


def _inproj_kernel(x_ref, g_ref, wa_ref, wu_ref, p_ref, u_ref):
    h = _rms(x_ref[...], g_ref[...]).astype(bf16)
    p_ref[...] = jnp.dot(h, wa_ref[...], preferred_element_type=f32)
    u_ref[...] = jnp.dot(h, wu_ref[...], preferred_element_type=f32)


def _inproj(x2, g, wa, wu):
    n = x2.shape[0]
    tm = _tile(n, 512)
    return pl.pallas_call(
        _inproj_kernel,
        grid=(n // tm,),
        in_specs=[
            pl.BlockSpec((tm, D_MODEL), lambda i: (i, 0)),
            pl.BlockSpec((1, D_MODEL), lambda i: (0, 0)),
            pl.BlockSpec((D_MODEL, PA_COLS), lambda i: (0, 0)),
            pl.BlockSpec((D_MODEL, 2 * C_CONV), lambda i: (0, 0)),
        ],
        out_specs=[
            pl.BlockSpec((tm, PA_COLS), lambda i: (i, 0)),
            pl.BlockSpec((tm, 2 * C_CONV), lambda i: (i, 0)),
        ],
        out_shape=[
            jax.ShapeDtypeStruct((n, PA_COLS), f32),
            jax.ShapeDtypeStruct((n, 2 * C_CONV), f32),
        ],
        compiler_params=_cparams(("parallel",)),
        name="inproj",
    )(x2, g, wa, wu)


def _prep_kernel(pc_ref, pp_ref, pn_ref, mup_ref, mun_ref, wup_ref, w0_ref, a0_ref, kkw_ref, ka_ref,
                 rk_ref, hsum_ref, r_o, v_o, kk_o, ew0_o, ew1_o, k0_o, k1_o, b0_o, b1_o, g_o, bonus_o):
    tt = pc_ref.shape[1]
    i = pl.program_id(1)
    nt = pl.num_programs(1)
    row = lax.broadcasted_iota(i32, (tt, 1), 0)
    first = i == 0
    last = i == nt - 1

    def shifted(c0, c1):
        p = pc_ref[0, :, c0:c1]
        prev_row = jnp.where(first, 0.0, pp_ref[0, 7:8, c0:c1])
        next_row = jnp.where(last, 0.0, pn_ref[0, 0:1, c0:c1])
        zp = jnp.where(row == 0, prev_row, pltpu.roll(p, 1, axis=0))
        zn = jnp.where(row == tt - 1, next_row, pltpu.roll(p, tt - 1, axis=0))
        return p + mup_ref[:, c0:c1] * (zp - p) + mun_ref[:, c0:c1] * (zn - p)

    r = shifted(0, D_RWKV)
    k = shifted(D_RWKV, 2 * D_RWKV)
    v = shifted(2 * D_RWKV, 3 * D_RWKV)
    zl = shifted(RKV_COLS, PA_COLS)

    lane = lax.broadcasted_iota(i32, (1, LORA_PAD), 1)
    act = jnp.where(lane < 2 * W_LORA, jnp.tanh(zl),
                    jnp.where(lane < 2 * W_LORA + 2 * A_LORA, zl, _sigmoid(zl)))
    pre = jnp.dot(act.astype(bf16), wup_ref[...], preferred_element_type=f32)

    hsum = hsum_ref[...]
    kkr = k * kkw_ref[...]
    ss = _mm_split_lhs(kkr * kkr, hsum, 2)
    kk = kkr / jnp.maximum(jnp.sqrt(ss), 1e-12)
    ka = ka_ref[...]
    ksum = None
    outs = ((ew0_o, k0_o, b0_o), (ew1_o, k1_o, b1_o))
    for d in range(2):
        x = pre[:, d * D_RWKV:(d + 1) * D_RWKV] + w0_ref[d:d + 1, :]
        ew = math.exp(-0.5) * _sigmoid(x)
        alpha = _sigmoid(pre[:, (2 + d) * D_RWKV:(3 + d) * D_RWKV] + a0_ref[d:d + 1, :])
        kd = k * (1.0 + (alpha - 1.0) * ka)
        ew_o, k_o, b_o = outs[d]
        ew_o[0] = ew
        k_o[0] = kd
        b_o[0] = kk * alpha
        ksum = kd if ksum is None else ksum + kd
    bonus = _mm_split_lhs(r * ksum * rk_ref[...], hsum, 2) * v
    r_o[0] = r
    v_o[0] = v
    kk_o[0] = kk
    g_o[0] = pre[:, 4 * D_RWKV:5 * D_RWKV]
    bonus_o[0] = bonus


def _prep(p3, mup, mun, wup, w0, a0, kkw, ka, rk, hsum):
    b, t, _ = p3.shape
    tt = _tile(t, 256)
    nt = t // tt
    hb = tt // 8
    full = lambda s: pl.BlockSpec(s, lambda bi, ti: tuple(0 for _ in s))
    o_spec = pl.BlockSpec((1, tt, D_RWKV), lambda bi, ti: (bi, ti, 0))
    o_shape = jax.ShapeDtypeStruct((b, t, D_RWKV), f32)
    return pl.pallas_call(
        _prep_kernel,
        grid=(b, nt),
        in_specs=[
            pl.BlockSpec((1, tt, PA_COLS), lambda bi, ti: (bi, ti, 0)),
            pl.BlockSpec((1, 8, PA_COLS), lambda bi, ti: (bi, jnp.maximum(ti * hb - 1, 0), 0)),
            pl.BlockSpec((1, 8, PA_COLS), lambda bi, ti: (bi, jnp.minimum((ti + 1) * hb, t // 8 - 1), 0)),
            full((1, PA_COLS)), full((1, PA_COLS)), full((LORA_PAD, UP_COLS)),
            full((2, D_RWKV)), full((2, D_RWKV)), full((1, D_RWKV)), full((1, D_RWKV)), full((1, D_RWKV)),
            full((D_RWKV, D_RWKV)),
        ],
        out_specs=[o_spec] * 11,
        out_shape=[o_shape] * 11,
        compiler_params=_cparams(("parallel", "parallel")),
        name="rwkv_prep",
    )(p3, p3, p3, mup, mun, wup, w0, a0, kkw, ka, rk, hsum)


def _scan_consts(reverse):
    L = CHUNK
    t = lax.broadcasted_iota(i32, (L, GROUP * L), 0)
    s = lax.broadcasted_iota(i32, (L, GROUP * L), 1) & (L - 1)
    strict = (s > t) if reverse else (s < t)
    eye = s == t
    incl = strict | eye
    tt = lax.broadcasted_iota(i32, (L, L), 0)
    ss = lax.broadcasted_iota(i32, (L, L), 1)
    tri = jnp.where((ss >= tt) if reverse else (ss <= tt), 1.0, 0.0).astype(bf16)
    rr = lax.broadcasted_iota(i32, (GROUP_LANES, GROUP_LANES), 0) // HEAD
    cc = lax.broadcasted_iota(i32, (GROUP_LANES, GROUP_LANES), 1) // HEAD
    same = rr == cc
    return strict, incl, eye, tri, (same, jnp.where(same, 1.0, 0.0).astype(bf16))


def _bd(x, bdmask):
    xb = x.astype(bf16)
    return jnp.concatenate([xb] * GROUP, axis=0) * bdmask[1]


def _wkv_chunks(chains):
    L = CHUNK
    n = len(chains)
    rng = range(n)
    r, v, kk, ew, k, b, consts, rev = (list(c) for c in zip(*chains))
    strict = [c[0] for c in consts]
    incl = [c[1] for c in consts]
    eye = [c[2] for c in consts]
    tri = [c[3] for c in consts]
    bdm = consts[0][4]
    ginc = [-_mm_split(tri[i], ew[i], 3) for i in rng]
    gref = [0.5 * (ginc[i][0:1] if rev[i] else ginc[i][L - 1:L]) for i in rng]
    egref = [jnp.exp(gref[i]) for i in rng]
    e1 = [jnp.exp(ginc[i] - gref[i]) for i in rng]
    e2 = [jnp.exp(gref[i] - ginc[i]) for i in rng]
    ea = [jnp.exp(ginc[i] + ew[i] - gref[i]) for i in rng]
    At = [-kk[i] * ea[i] for i in rng]
    Rt = [r[i] * e1[i] for i in rng]
    Kt = [k[i] * e2[i] for i in rng]
    Bt = [b[i] * e2[i] for i in rng]
    Be = [Bt[i] * egref[i] for i in rng]
    AR = [jnp.concatenate([At[i], Rt[i]], axis=0) for i in rng]
    pb = [_mm(AR[i], _bd(Bt[i], bdm), NT) for i in rng]
    pk = [_mm(AR[i], _bd(Kt[i], bdm), NT) for i in rng]
    Mab = [jnp.where(strict[i], pb[i][:L], 0.0) for i in rng]
    Arb = [jnp.where(incl[i], pb[i][L:], 0.0) for i in rng]
    Mak = [jnp.where(strict[i], pk[i][:L], 0.0) for i in rng]
    Ark = [jnp.where(incl[i], pk[i][L:], 0.0) for i in rng]
    vbd = [_bd(v[i], bdm) for i in rng]
    MakV = [_mm(Mak[i], vbd[i]) for i in rng]
    X = [jnp.where(eye[i], 1.0, Mab[i]) for i in rng]
    P = [_mm(Mab[i], _bd(Mab[i], bdm)) for i in rng]
    for step in range(4):
        XP = [_mm(jnp.concatenate([X[i], P[i]], axis=0), _bd(P[i], bdm)) for i in rng]
        X = [X[i] + XP[i][:L] for i in rng]
        P = [XP[i][L:] for i in rng]
    X = [X[i] + _mm(X[i], _bd(P[i], bdm)) for i in rng]
    W1 = [_mm(X[i], _bd(At[i] * egref[i], bdm)) for i in rng]
    W2 = [_mm(X[i], _bd(MakV[i], bdm)) for i in rng]
    Q = [Rt[i] * egref[i] + _mm(Arb[i], _bd(W1[i], bdm)) for i in rng]
    Y0 = [_mm(Arb[i], _bd(W2[i], bdm)) + _mm(Ark[i], vbd[i]) for i in rng]
    Pm = [jnp.where(bdm[0], _mm(W1[i], Be[i], TN), 0.0) for i in rng]
    Qm = [jnp.where(bdm[0], _mm(jnp.concatenate([W2[i], v[i]], axis=0),
                                jnp.concatenate([Be[i], Kt[i] * egref[i]], axis=0), TN), 0.0) for i in rng]
    D = [egref[i] * egref[i] for i in rng]
    return Q, Y0, Pm, Qm, D


def _scan_kernel(rf, vf, kkf, ewf, kf, bf, rb, vb, kkb, ewb, kb, bb, yf_ref, yb_ref, s_ref):
    tb = rf.shape[1]
    nch = tb // CHUNK

    @pl.when(pl.program_id(1) == 0)
    def _():
        s_ref[...] = jnp.zeros_like(s_ref)

    consts_f = _scan_consts(False)
    consts_b = _scan_consts(True)

    def body(j, carry):
        items, dests = [], []
        for sub in range(SCAN_UNROLL):
            for reverse, refs, y_ref, consts in ((False, (rf, vf, kkf, ewf, kf, bf), yf_ref, consts_f),
                                                 (True, (rb, vb, kkb, ewb, kb, bb), yb_ref, consts_b)):
                step = j * SCAN_UNROLL + sub
                cj = (nch - 1 - step) if reverse else step
                rows = pl.ds(pl.multiple_of(cj * CHUNK, CHUNK), CHUNK)
                for g in range(N_GROUPS):
                    lanes = slice(g * GROUP_LANES, (g + 1) * GROUP_LANES)
                    items.append(tuple(ref[0, rows, lanes] for ref in refs) + (consts, reverse))
                    dests.append((y_ref, rows, lanes, (2 if reverse else 0) + g))
        Q, Y0, Pm, Qm, D = _wkv_chunks(items)
        S = [s_ref[si] for si in range(2 * N_GROUPS)]
        for i, (y_ref, rows, lanes, si) in enumerate(dests):
            y_ref[0, rows, lanes] = _mm(Q[i], S[si], NT) + Y0[i]
            S[si] = S[si] * D[i] + _mm(S[si], Pm[i]) + Qm[i]
        for si in range(2 * N_GROUPS):
            s_ref[si] = S[si]
        return carry

    lax.fori_loop(0, nch // SCAN_UNROLL, body, 0)


def _scan(r, v, kk, ew0, ew1, k0, k1, b0, b1):
    b, t, _ = r.shape
    tb = _tile(t, SCAN_BLOCK)
    assert tb % (CHUNK * SCAN_UNROLL) == 0
    nb = t // tb
    fwd = pl.BlockSpec((1, tb, D_RWKV), lambda bi, ti: (bi, ti, 0))
    bwd = pl.BlockSpec((1, tb, D_RWKV), lambda bi, ti: (bi, nb - 1 - ti, 0))
    o_shape = jax.ShapeDtypeStruct((b, t, D_RWKV), f32)
    return pl.pallas_call(
        _scan_kernel,
        grid=(b, nb),
        in_specs=[fwd] * 6 + [bwd] * 6,
        out_specs=[fwd, bwd],
        out_shape=[o_shape, o_shape],
        scratch_shapes=[pltpu.VMEM((2 * N_GROUPS, GROUP_LANES, GROUP_LANES), f32)],
        compiler_params=_cparams(("parallel", "arbitrary")),
        name="wkv_scan",
    )(r, v, kk, ew0, k0, b0, r, v, kk, ew1, k1, b1)


CONV_HALO = 16


def _conv_kernel(uc_ref, up_ref, un_ref, kern_ref, bias_ref, lnw_ref, lnb_ref, o_ref, ext_ref):
    tt = uc_ref.shape[1]
    i = pl.program_id(1)
    nt = pl.num_programs(1)

    def glu(u):
        return u[:, :C_CONV] * _sigmoid(u[:, C_CONV:])

    ext_ref[0:CONV_HALO, :] = jnp.where(i == 0, 0.0, glu(up_ref[0]))
    ext_ref[CONV_HALO:CONV_HALO + tt, :] = glu(uc_ref[0])
    ext_ref[CONV_HALO + tt:, :] = jnp.where(i == nt - 1, 0.0, glu(un_ref[0]))
    acc = jnp.zeros((tt, C_CONV), f32) + bias_ref[...]
    half = CONV_W // 2
    for j in range(CONV_W):
        off = CONV_HALO - half + j
        acc = acc + kern_ref[j:j + 1, :] * ext_ref[off:off + tt, :]
    mu = jnp.mean(acc, axis=-1, keepdims=True)
    xc = acc - mu
    var = jnp.mean(xc * xc, axis=-1, keepdims=True)
    y = xc * lax.rsqrt(var + LN_EPS) * lnw_ref[...] + lnb_ref[...]
    o_ref[0] = y * _sigmoid(y)


def _conv(u3, kern, bias, lnw, lnb):
    b, t, _ = u3.shape
    tt = _tile(t, 512)
    hb = tt // CONV_HALO
    full = lambda s: pl.BlockSpec(s, lambda bi, ti: tuple(0 for _ in s))
    return pl.pallas_call(
        _conv_kernel,
        grid=(b, t // tt),
        in_specs=[
            pl.BlockSpec((1, tt, 2 * C_CONV), lambda bi, ti: (bi, ti, 0)),
            pl.BlockSpec((1, CONV_HALO, 2 * C_CONV), lambda bi, ti: (bi, jnp.maximum(ti * hb - 1, 0), 0)),
            pl.BlockSpec((1, CONV_HALO, 2 * C_CONV),
                         lambda bi, ti: (bi, jnp.minimum((ti + 1) * hb, t // CONV_HALO - 1), 0)),
            full((CONV_W, C_CONV)), full((1, C_CONV)), full((1, C_CONV)), full((1, C_CONV)),
        ],
        out_specs=pl.BlockSpec((1, tt, C_CONV), lambda bi, ti: (bi, ti, 0)),
        out_shape=jax.ShapeDtypeStruct((b, t, C_CONV), f32),
        scratch_shapes=[pltpu.VMEM((tt + 2 * CONV_HALO, C_CONV), f32)],
        compiler_params=_cparams(("parallel", "parallel")),
        name="conv_branch",
    )(u3, u3, u3, kern, bias, lnw, lnb)


def _merge_kernel(x_ref, yf_ref, yb_ref, g_ref, bonus_ref, cb_ref, nmix_ref, wg_ref, lnxw_ref, lnxb_ref,
                  havg_ref, wor_ref, woc_ref, wout_ref, nffn_ref, rwt_ref, rb_ref,
                  x1_ref, h2_ref, aff_ref):
    x = x_ref[...]
    h = _rms(x, nmix_ref[...]).astype(bf16)
    gates = _sigmoid(jnp.dot(h, wg_ref[...], preferred_element_type=f32))
    wkv = yf_ref[...] + yb_ref[...]
    havg = havg_ref[...]
    mu = _mm_split_lhs(wkv, havg, 2)
    xc = wkv - mu
    var = _mm_split_lhs(xc * xc, havg, 2)
    y = (xc * lax.rsqrt(var + GN_EPS) * lnxw_ref[...] + lnxb_ref[...] + bonus_ref[...]) * g_ref[...]
    y_a = jnp.dot(y.astype(bf16), wor_ref[...], preferred_element_type=f32)
    y_b = jnp.dot(cb_ref[...].astype(bf16), woc_ref[...], preferred_element_type=f32)
    m = gates[:, :D_MODEL] * y_a + gates[:, D_MODEL:] * y_b
    x1 = x + jnp.dot(m.astype(bf16), wout_ref[...], preferred_element_type=f32)
    x1_ref[...] = x1
    h2 = _rms(x1, nffn_ref[...])
    h2_ref[...] = h2
    logits = lax.dot_general(rwt_ref[...], h2.astype(bf16), NT, preferred_element_type=f32) + rb_ref[...]
    mx = jnp.max(logits, axis=0, keepdims=True)
    e = jnp.exp(logits - mx)
    aff_ref[...] = e / jnp.sum(e, axis=0, keepdims=True)


def _merge(x2, yf, yb, g, bonus, cb, nmix, wg, lnxw, lnxb, havg, wor, woc, wout, nffn, rwt, rb):
    n = x2.shape[0]
    tm = _tile(n, 512)
    row = lambda c: pl.BlockSpec((tm, c), lambda i: (i, 0))
    full = lambda s: pl.BlockSpec(s, lambda i: tuple(0 for _ in s))
    return pl.pallas_call(
        _merge_kernel,
        grid=(n // tm,),
        in_specs=[
            row(D_MODEL), row(D_RWKV), row(D_RWKV), row(D_RWKV), row(D_RWKV), row(C_CONV),
            full((1, D_MODEL)), full((D_MODEL, 2 * D_MODEL)), full((1, D_RWKV)), full((1, D_RWKV)),
            full((D_RWKV, D_RWKV)), full((D_RWKV, D_MODEL)), full((C_CONV, D_MODEL)),
            full((D_MODEL, D_MODEL)), full((1, D_MODEL)), full((N_EXPERTS, D_MODEL)), full((N_EXPERTS, 1)),
        ],
        out_specs=[row(D_MODEL), row(D_MODEL), pl.BlockSpec((N_EXPERTS, tm), lambda i: (0, i))],
        out_shape=[
            jax.ShapeDtypeStruct((n, D_MODEL), f32),
            jax.ShapeDtypeStruct((n, D_MODEL), f32),
            jax.ShapeDtypeStruct((N_EXPERTS, n), f32),
        ],
        compiler_params=_cparams(("parallel",)),
        name="merge_router",
    )(x2, yf, yb, g, bonus, cb, nmix, wg, lnxw, lnxb, havg, wor, woc, wout, nffn, rwt, rb)


def _thr_kernel(aff_ref, thr_ref, *, cap):
    def count_ge(mid):
        bits = pltpu.bitcast(aff_ref[...], i32)
        return jnp.sum(jnp.where(bits >= mid, 1.0, 0.0), axis=1, keepdims=True)

    def body(_, c):
        lo, hi = c
        mid = lo + ((hi - lo + 1) >> 1)
        ok = count_ge(mid) >= cap
        return jnp.where(ok, mid, lo), jnp.where(ok, hi, mid - 1)

    lo0 = jnp.zeros((N_EXPERTS, 1), i32)
    hi0 = jnp.full((N_EXPERTS, 1), 0x3F800000, i32)
    lo, _ = lax.fori_loop(0, 31, body, (lo0, hi0))
    thr_ref[...] = jnp.broadcast_to(lo, thr_ref.shape)


def _threshold(aff_t, cap):
    e, n = aff_t.shape
    return pl.pallas_call(
        functools.partial(_thr_kernel, cap=float(cap)),
        in_specs=[pl.BlockSpec((e, n), lambda: (0, 0))],
        out_specs=pl.BlockSpec((e, LANES_V7X), lambda: (0, 0)),
        out_shape=jax.ShapeDtypeStruct((e, LANES_V7X), i32),
        compiler_params=pltpu.CompilerParams(vmem_limit_bytes=VMEM_LIMIT_V7X),
        name="route_threshold",
    )(aff_t)


def _select(aff_t, cap):
    thr = _threshold(aff_t, cap)[:, :1]
    bits = lax.bitcast_convert_type(aff_t, i32)
    gt = bits > thr
    eq = bits == thr
    need = cap - jnp.sum(gt, axis=1, keepdims=True)
    sel = gt | (eq & (jnp.cumsum(eq, axis=1) <= need))
    cum = jnp.cumsum(sel, axis=1)
    slots = jnp.arange(1, cap + 1, dtype=cum.dtype)
    idx = jax.vmap(lambda c: jnp.searchsorted(c, slots, side="left"))(cum).astype(i32)
    gate = jnp.take_along_axis(aff_t, idx, axis=1)
    return idx, gate


def _expert_kernel(idx_ref, idxn_ref, gate_ref, h2_hbm, acc_in_hbm, wg_ref, wu_ref, wd_ref, acc_hbm,
                   xbuf, abuf, sem):
    del acc_in_hbm
    tc = abuf.shape[0]
    s = pl.program_id(0)
    last = pl.num_programs(0) - 1
    slot = s % 2

    def rows(fn):
        def body(i, c):
            fn(i)
            return c
        lax.fori_loop(0, tc, body, 0, unroll=8)

    def start_x(ref, sl):
        rows(lambda i: pltpu.make_async_copy(
            h2_hbm.at[pl.ds(ref[0, 0, i], 1)], xbuf.at[sl, pl.ds(i, 1)], sem.at[sl]).start())

    def wait_tile(buf, k):
        pltpu.make_async_copy(h2_hbm.at[pl.ds(0, tc)], buf, sem.at[k]).wait()

    @pl.when(s == 0)
    def _():
        start_x(idx_ref, 0)

    @pl.when(s < last)
    def _():
        start_x(idxn_ref, 1 - slot)

    @pl.when(s > 0)
    def _():
        wait_tile(abuf, 3)

    rows(lambda i: pltpu.make_async_copy(
        acc_hbm.at[pl.ds(idx_ref[0, 0, i], 1)], abuf.at[pl.ds(i, 1)], sem.at[2]).start())
    wait_tile(xbuf.at[slot], slot)
    xe = xbuf[slot].astype(bf16)
    hg = jnp.dot(xe, wg_ref[0], preferred_element_type=f32)
    hu = jnp.dot(xe, wu_ref[0], preferred_element_type=f32)
    hid = (hg * _sigmoid(hg) * hu).astype(bf16)
    ye = jnp.dot(hid, wd_ref[0], preferred_element_type=f32) * gate_ref[0]
    wait_tile(abuf, 2)
    abuf[...] = abuf[...] + ye
    rows(lambda i: pltpu.make_async_copy(
        abuf.at[pl.ds(i, 1)], acc_hbm.at[pl.ds(idx_ref[0, 0, i], 1)], sem.at[3]).start())

    @pl.when(s == last)
    def _():
        wait_tile(abuf, 3)


def _experts(idx, gate, h2, x1, wg, wu, wd):
    e, cap = idx.shape
    n = h2.shape[0]
    tc = _tile(cap, 256)
    nt = cap // tc
    steps = e * nt
    wspec = pl.BlockSpec((1, D_MODEL, D_FF), lambda si: (si // nt, 0, 0))
    idx3 = idx.reshape(steps, 1, tc)
    return pl.pallas_call(
        _expert_kernel,
        grid=(steps,),
        in_specs=[
            pl.BlockSpec((1, 1, tc), lambda si: (si, 0, 0), memory_space=pltpu.SMEM),
            pl.BlockSpec((1, 1, tc), lambda si: (jnp.minimum(si + 1, steps - 1), 0, 0), memory_space=pltpu.SMEM),
            pl.BlockSpec((1, tc, 1), lambda si: (si, 0, 0)),
            pl.BlockSpec(memory_space=pl.ANY),
            pl.BlockSpec(memory_space=pl.ANY),
            wspec, wspec,
            pl.BlockSpec((1, D_FF, D_MODEL), lambda si: (si // nt, 0, 0)),
        ],
        out_specs=pl.BlockSpec(memory_space=pl.ANY),
        out_shape=jax.ShapeDtypeStruct((n, D_MODEL), f32),
        input_output_aliases={4: 0},
        scratch_shapes=[pltpu.VMEM((2, tc, D_MODEL), f32), pltpu.VMEM((tc, D_MODEL), f32),
                        pltpu.SemaphoreType.DMA((4,))],
        compiler_params=_cparams(("arbitrary",)),
        name="expert_ffn",
    )(idx3, idx3, gate.reshape(steps, tc, 1), h2, x1, wg, wu, wd)


def _final_kernel(x_ref, g_ref, o_ref):
    o_ref[...] = _rms(x_ref[...], g_ref[...])


def _final(x2, g):
    n = x2.shape[0]
    tm = _tile(n, 1024)
    return pl.pallas_call(
        _final_kernel,
        grid=(n // tm,),
        in_specs=[pl.BlockSpec((tm, D_MODEL), lambda i: (i, 0)), pl.BlockSpec((1, D_MODEL), lambda i: (0, 0))],
        out_specs=pl.BlockSpec((tm, D_MODEL), lambda i: (i, 0)),
        out_shape=jax.ShapeDtypeStruct((n, D_MODEL), f32),
        compiler_params=_cparams(("parallel",)),
        name="final_norm",
    )(x2, g)


def _prepare_params(norm_mix, w_in, shift_mu_prev, shift_mu_next, decay_w0, decay_up, iclr_a0, iclr_up,
                    gate_up, k_k, k_a, r_k, lnx_w, lnx_b, w_o_rwkv, dw_kernel, dw_bias, conv_ln_w, conv_ln_b,
                    w_o_conv, w_out, norm_ffn, router_w, router_b, exp_w_gate, exp_w_up, exp_w_down, norm_final):
    l = 0
    rwkv_cols = RKV_COLS + LORA_COLS
    w = w_in[l]
    pad = jnp.zeros((D_MODEL, LORA_PAD - LORA_COLS), f32)
    wa = jnp.concatenate([w[:, :rwkv_cols], pad], axis=1).astype(bf16)
    wu = w[:, rwkv_cols:rwkv_cols + 2 * C_CONV].astype(bf16)
    wgate = w[:, rwkv_cols + 2 * C_CONV:].astype(bf16)
    padv = jnp.zeros((LORA_PAD - LORA_COLS,), f32)
    mup = jnp.concatenate([shift_mu_prev[l], padv])[None, :]
    mun = jnp.concatenate([shift_mu_next[l], padv])[None, :]
    wup = jnp.zeros((LORA_PAD, UP_COLS), f32)
    wup = wup.at[0:W_LORA, 0:D_RWKV].set(decay_up[l, 0])
    wup = wup.at[W_LORA:2 * W_LORA, D_RWKV:2 * D_RWKV].set(decay_up[l, 1])
    o = 2 * W_LORA
    wup = wup.at[o:o + A_LORA, 2 * D_RWKV:3 * D_RWKV].set(iclr_up[l, 0])
    wup = wup.at[o + A_LORA:o + 2 * A_LORA, 3 * D_RWKV:4 * D_RWKV].set(iclr_up[l, 1])
    o = 2 * W_LORA + 2 * A_LORA
    wup = wup.at[o:o + G_LORA, 4 * D_RWKV:5 * D_RWKV].set(gate_up[l])
    head = jnp.arange(D_RWKV) // HEAD
    same = (head[:, None] == head[None, :])
    return dict(
        norm_mix=norm_mix[l][None, :], wa=wa, wu=wu, wgate=wgate, mup=mup, mun=mun, wup=wup.astype(bf16),
        w0=decay_w0[l], a0=iclr_a0[l], kkw=k_k[l][None, :], ka=k_a[l][None, :], rk=r_k[l].reshape(1, D_RWKV),
        hsum=same.astype(bf16), havg=(same.astype(f32) / HEAD).astype(bf16),
        lnxw=lnx_w[l][None, :], lnxb=lnx_b[l][None, :], wor=w_o_rwkv[l].astype(bf16),
        kern=dw_kernel[l], cbias=dw_bias[l][None, :], clnw=conv_ln_w[l][None, :], clnb=conv_ln_b[l][None, :],
        woc=w_o_conv[l].astype(bf16), wout=w_out[l].astype(bf16), nffn=norm_ffn[l][None, :],
        rwt=router_w[l].T.astype(bf16), rb=router_b[l][:, None],
        ewg=exp_w_gate[l].astype(bf16), ewu=exp_w_up[l].astype(bf16), ewd=exp_w_down[l].astype(bf16),
        nfinal=norm_final[None, :],
    )


def _trunk(x, P):
    b, t, d = x.shape
    n = b * t
    x2 = x.reshape(n, d)
    p, u = _inproj(x2, P["norm_mix"], P["wa"], P["wu"])
    r, v, kk, ew0, ew1, k0, k1, b0, b1, g, bonus = _prep(
        p.reshape(b, t, PA_COLS), P["mup"], P["mun"], P["wup"], P["w0"], P["a0"], P["kkw"], P["ka"], P["rk"],
        P["hsum"])
    yf, yb = _scan(r, v, kk, ew0, ew1, k0, k1, b0, b1)
    cb = _conv(u.reshape(b, t, 2 * C_CONV), P["kern"], P["cbias"], P["clnw"], P["clnb"])
    flat = lambda a: a.reshape(n, a.shape[-1])
    x1, h2, aff_t = _merge(x2, flat(yf), flat(yb), flat(g), flat(bonus), flat(cb), P["norm_mix"], P["wgate"],
                           P["lnxw"], P["lnxb"], P["havg"], P["wor"], P["woc"], P["wout"], P["nffn"],
                           P["rwt"], P["rb"])
    cap = max(1, CAPACITY_FACTOR * n // N_EXPERTS)
    idx, gate = _select(aff_t, cap)
    x2o = _experts(idx, gate, h2, x1, P["ewg"], P["ewu"], P["ewd"])
    return _final(x2o, P["nfinal"]).reshape(b, t, d)


def kernel(x_prompt, x_sample, norm_mix, w_in, shift_mu_prev, shift_mu_next, decay_w0, decay_up, iclr_a0, iclr_up, gate_up, k_k, k_a, r_k, lnx_w, lnx_b, w_o_rwkv, dw_kernel, dw_bias, conv_ln_w, conv_ln_b, w_o_conv, w_out, norm_ffn, router_w, router_b, exp_w_gate, exp_w_up, exp_w_down, norm_final):
    P = _prepare_params(norm_mix, w_in, shift_mu_prev, shift_mu_next, decay_w0, decay_up, iclr_a0, iclr_up,
                        gate_up, k_k, k_a, r_k, lnx_w, lnx_b, w_o_rwkv, dw_kernel, dw_bias, conv_ln_w,
                        conv_ln_b, w_o_conv, w_out, norm_ffn, router_w, router_b, exp_w_gate, exp_w_up,
                        exp_w_down, norm_final)
    return (_trunk(x_prompt, P), _trunk(x_sample, P))
```

```python
import functools
import math

import jax
import jax.numpy as jnp
from jax import lax
from jax.experimental import pallas as pl
from jax.experimental.pallas import tpu as pltpu

f32 = jnp.float32
bf16 = jnp.bfloat16
i32 = jnp.int32

D_MODEL = 1024
D_RWKV = 512
HEAD = 64
W_LORA = 32
A_LORA = 32
G_LORA = 96
C_CONV = 512
CONV_W = 31
N_EXPERTS = 16
D_FF = 1024
CAPACITY_FACTOR = 2
GN_EPS = 64e-5
LN_EPS = 1e-5
RMS_EPS = 1e-6
LORA_COLS = 2 * W_LORA + 2 * A_LORA + G_LORA
LORA_PAD = 256
RKV_COLS = 3 * D_RWKV
PA_COLS = RKV_COLS + LORA_PAD
UP_COLS = 5 * D_RWKV

LANES_V7X = 128
SUBLANES_V7X = 8
MXU_DIM_V7X = 256
VMEM_LIMIT_V7X = 56 * 1024 * 1024

CHUNK = 64
SCAN_UNROLL = 2
SCAN_BLOCK = 512
GROUP = MXU_DIM_V7X // HEAD
GROUP_LANES = GROUP * HEAD
N_GROUPS = D_RWKV // GROUP_LANES

NN = (((1,), (0,)), ((), ()))
NT = (((1,), (1,)), ((), ()))
TN = (((0,), (0,)), ((), ()))


def _mm(a, b, dims=NN):
    return lax.dot_general(a.astype(bf16), b.astype(bf16), dims, preferred_element_type=f32)


def _mm_split(a, x, n):
    out = None
    rem = x
    for _ in range(n):
        hi = rem.astype(bf16)
        t = lax.dot_general(a, hi, NN, preferred_element_type=f32)
        out = t if out is None else out + t
        rem = rem - hi.astype(f32)
    return out


def _mm_split_lhs(x, a, n):
    out = None
    rem = x
    for _ in range(n):
        hi = rem.astype(bf16)
        t = lax.dot_general(hi, a, NN, preferred_element_type=f32)
        out = t if out is None else out + t
        rem = rem - hi.astype(f32)
    return out


def _sigmoid(x):
    return 1.0 / (1.0 + jnp.exp(-x))


def _rms(x, g):
    return x * lax.rsqrt(jnp.mean(x * x, axis=-1, keepdims=True) + RMS_EPS) * g


def _cparams(sem):
    return pltpu.CompilerParams(dimension_semantics=sem, vmem_limit_bytes=VMEM_LIMIT_V7X)


def _tile(n, pref):
    t = min(n, pref)
    assert n % t == 0, (n, pref)
    return t


def _inproj_kernel(x_ref, g_ref, wa_ref, wu_ref, p_ref, u_ref):
    h = _rms(x_ref[...], g_ref[...]).astype(bf16)
    p_ref[...] = jnp.dot(h, wa_ref[...], preferred_element_type=f32).astype(bf16)
    u_ref[...] = jnp.dot(h, wu_ref[...], preferred_element_type=f32).astype(bf16)


def _inproj(x2, g, wa, wu):
    n = x2.shape[0]
    tm = _tile(n, 512)
    return pl.pallas_call(
        _inproj_kernel,
        grid=(n // tm,),
        in_specs=[
            pl.BlockSpec((tm, D_MODEL), lambda i: (i, 0)),
            pl.BlockSpec((1, D_MODEL), lambda i: (0, 0)),
            pl.BlockSpec((D_MODEL, PA_COLS), lambda i: (0, 0)),
            pl.BlockSpec((D_MODEL, 2 * C_CONV), lambda i: (0, 0)),
        ],
        out_specs=[
            pl.BlockSpec((tm, PA_COLS), lambda i: (i, 0)),
            pl.BlockSpec((tm, 2 * C_CONV), lambda i: (i, 0)),
        ],
        out_shape=[
            jax.ShapeDtypeStruct((n, PA_COLS), bf16),
            jax.ShapeDtypeStruct((n, 2 * C_CONV), bf16),
        ],
        compiler_params=_cparams(("parallel",)),
        name="inproj",
    )(x2, g, wa, wu)


PREP_HALO = 16


def _prep_kernel(pc_ref, pp_ref, pn_ref, mup_ref, mun_ref, wup_ref, w0_ref, a0_ref, kkw_ref, ka_ref,
                 rk_ref, hsum_ref, r_o, v_o, kk_o, ew0_o, ew1_o, k0_o, k1_o, b0_o, b1_o, g_o, bonus_o):
    tt = pc_ref.shape[1]
    i = pl.program_id(1)
    nt = pl.num_programs(1)
    row = lax.broadcasted_iota(i32, (tt, 1), 0)
    first = i == 0
    last = i == nt - 1

    def shifted(c0, c1):
        p = pc_ref[0, :, c0:c1].astype(f32)
        prev_row = jnp.where(first, 0.0, pp_ref[0, PREP_HALO - 1:PREP_HALO, c0:c1].astype(f32))
        next_row = jnp.where(last, 0.0, pn_ref[0, 0:1, c0:c1].astype(f32))
        zp = jnp.where(row == 0, prev_row, pltpu.roll(p, 1, axis=0))
        zn = jnp.where(row == tt - 1, next_row, pltpu.roll(p, tt - 1, axis=0))
        return p + mup_ref[:, c0:c1] * (zp - p) + mun_ref[:, c0:c1] * (zn - p)

    r = shifted(0, D_RWKV)
    k = shifted(D_RWKV, 2 * D_RWKV)
    v = shifted(2 * D_RWKV, 3 * D_RWKV)
    zl = shifted(RKV_COLS, PA_COLS)

    lane = lax.broadcasted_iota(i32, (1, LORA_PAD), 1)
    act = jnp.where(lane < 2 * W_LORA, jnp.tanh(zl),
                    jnp.where(lane < 2 * W_LORA + 2 * A_LORA, zl, _sigmoid(zl)))
    pre = jnp.dot(act.astype(bf16), wup_ref[...], preferred_element_type=f32)

    hsum = hsum_ref[...]
    kkr = k * kkw_ref[...]
    ss = _mm_split_lhs(kkr * kkr, hsum, 2)
    kk = kkr / jnp.maximum(jnp.sqrt(ss), 1e-12)
    ka = ka_ref[...]
    ksum = None
    outs = ((ew0_o, k0_o, b0_o), (ew1_o, k1_o, b1_o))
    for d in range(2):
        x = pre[:, d * D_RWKV:(d + 1) * D_RWKV] + w0_ref[d:d + 1, :]
        ew = math.exp(-0.5) * _sigmoid(x)
        alpha = _sigmoid(pre[:, (2 + d) * D_RWKV:(3 + d) * D_RWKV] + a0_ref[d:d + 1, :])
        kd = k * (1.0 + (alpha - 1.0) * ka)
        ew_o, k_o, b_o = outs[d]
        ew_o[0] = ew
        k_o[0] = kd.astype(bf16)
        b_o[0] = (kk * alpha).astype(bf16)
        ksum = kd if ksum is None else ksum + kd
    bonus = _mm_split_lhs(r * ksum * rk_ref[...], hsum, 2) * v
    r_o[0] = r.astype(bf16)
    v_o[0] = v.astype(bf16)
    kk_o[0] = kk.astype(bf16)
    g_o[0] = pre[:, 4 * D_RWKV:5 * D_RWKV].astype(bf16)
    bonus_o[0] = bonus.astype(bf16)


def _prep(p3, mup, mun, wup, w0, a0, kkw, ka, rk, hsum):
    b, t, _ = p3.shape
    tt = _tile(t, 256)
    nt = t // tt
    hb = tt // PREP_HALO
    full = lambda s: pl.BlockSpec(s, lambda bi, ti: tuple(0 for _ in s))
    o_spec = pl.BlockSpec((1, tt, D_RWKV), lambda bi, ti: (bi, ti, 0))
    o_lo = jax.ShapeDtypeStruct((b, t, D_RWKV), bf16)
    o_hi = jax.ShapeDtypeStruct((b, t, D_RWKV), f32)
    return pl.pallas_call(
        _prep_kernel,
        grid=(b, nt),
        in_specs=[
            pl.BlockSpec((1, tt, PA_COLS), lambda bi, ti: (bi, ti, 0)),
            pl.BlockSpec((1, PREP_HALO, PA_COLS), lambda bi, ti: (bi, jnp.maximum(ti * hb - 1, 0), 0)),
            pl.BlockSpec((1, PREP_HALO, PA_COLS),
                         lambda bi, ti: (bi, jnp.minimum((ti + 1) * hb, t // PREP_HALO - 1), 0)),
            full((1, PA_COLS)), full((1, PA_COLS)), full((LORA_PAD, UP_COLS)),
            full((2, D_RWKV)), full((2, D_RWKV)), full((1, D_RWKV)), full((1, D_RWKV)), full((1, D_RWKV)),
            full((D_RWKV, D_RWKV)),
        ],
        out_specs=[o_spec] * 11,
        out_shape=[o_lo, o_lo, o_lo, o_hi, o_hi] + [o_lo] * 6,
        compiler_params=_cparams(("parallel", "parallel")),
        name="rwkv_prep",
    )(p3, p3, p3, mup, mun, wup, w0, a0, kkw, ka, rk, hsum)


def _scan_consts(reverse):
    L = CHUNK
    t = lax.broadcasted_iota(i32, (L, GROUP * L), 0)
    s = lax.broadcasted_iota(i32, (L, GROUP * L), 1) & (L - 1)
    strict = (s > t) if reverse else (s < t)
    eye = s == t
    incl = strict | eye
    tt = lax.broadcasted_iota(i32, (L, L), 0)
    ss = lax.broadcasted_iota(i32, (L, L), 1)
    tri = jnp.where((ss >= tt) if reverse else (ss <= tt), 1.0, 0.0).astype(bf16)
    rr = lax.broadcasted_iota(i32, (GROUP_LANES, GROUP_LANES), 0) // HEAD
    cc = lax.broadcasted_iota(i32, (GROUP_LANES, GROUP_LANES), 1) // HEAD
    same = rr == cc
    return strict, incl, eye, tri, (same, jnp.where(same, 1.0, 0.0).astype(bf16))


def _bd(x, bdmask):
    xb = x.astype(bf16)
    return jnp.concatenate([xb] * GROUP, axis=0) * bdmask[1]


def _wkv_chunks(chains):
    L = CHUNK
    n = len(chains)
    rng = range(n)
    r, v, kk, ew, k, b, consts, rev = (list(c) for c in zip(*chains))
    strict = [c[0] for c in consts]
    incl = [c[1] for c in consts]
    eye = [c[2] for c in consts]
    tri = [c[3] for c in consts]
    bdm = consts[0][4]
    ginc = [-_mm_split(tri[i], ew[i], 3) for i in rng]
    gref = [0.5 * (ginc[i][0:1] if rev[i] else ginc[i][L - 1:L]) for i in rng]
    egref = [jnp.exp(gref[i]) for i in rng]
    e1 = [jnp.exp(ginc[i] - gref[i]) for i in rng]
    e2 = [jnp.exp(gref[i] - ginc[i]) for i in rng]
    ea = [jnp.exp(ginc[i] + ew[i] - gref[i]) for i in rng]
    At = [-kk[i] * ea[i] for i in rng]
    Rt = [r[i] * e1[i] for i in rng]
    Kt = [k[i] * e2[i] for i in rng]
    Bt = [b[i] * e2[i] for i in rng]
    Be = [Bt[i] * egref[i] for i in rng]
    AR = [jnp.concatenate([At[i], Rt[i]], axis=0) for i in rng]
    pb = [_mm(AR[i], _bd(Bt[i], bdm), NT) for i in rng]
    pk = [_mm(AR[i], _bd(Kt[i], bdm), NT) for i in rng]
    Mab = [jnp.where(strict[i], pb[i][:L], 0.0) for i in rng]
    Arb = [jnp.where(incl[i], pb[i][L:], 0.0) for i in rng]
    Mak = [jnp.where(strict[i], pk[i][:L], 0.0) for i in rng]
    Ark = [jnp.where(incl[i], pk[i][L:], 0.0) for i in rng]
    vbd = [_bd(v[i], bdm) for i in rng]
    MakV = [_mm(Mak[i], vbd[i]) for i in rng]
    X = [jnp.where(eye[i], 1.0, Mab[i]) for i in rng]
    P = [_mm(Mab[i], _bd(Mab[i], bdm)) for i in rng]
    for step in range(4):
        XP = [_mm(jnp.concatenate([X[i], P[i]], axis=0), _bd(P[i], bdm)) for i in rng]
        X = [X[i] + XP[i][:L] for i in rng]
        P = [XP[i][L:] for i in rng]
    X = [X[i] + _mm(X[i], _bd(P[i], bdm)) for i in rng]
    W1 = [_mm(X[i], _bd(At[i] * egref[i], bdm)) for i in rng]
    W2 = [_mm(X[i], _bd(MakV[i], bdm)) for i in rng]
    Q = [Rt[i] * egref[i] + _mm(Arb[i], _bd(W1[i], bdm)) for i in rng]
    Y0 = [_mm(Arb[i], _bd(W2[i], bdm)) + _mm(Ark[i], vbd[i]) for i in rng]
    Pm = [jnp.where(bdm[0], _mm(W1[i], Be[i], TN), 0.0) for i in rng]
    Qm = [jnp.where(bdm[0], _mm(jnp.concatenate([W2[i], v[i]], axis=0),
                                jnp.concatenate([Be[i], Kt[i] * egref[i]], axis=0), TN), 0.0) for i in rng]
    D = [egref[i] * egref[i] for i in rng]
    return Q, Y0, Pm, Qm, D


def _scan_kernel(rf, vf, kkf, ewf, kf, bf, rb, vb, kkb, ewb, kb, bb, yf_ref, yb_ref, s_ref):
    tb = rf.shape[1]
    nch = tb // CHUNK

    @pl.when(pl.program_id(1) == 0)
    def _():
        s_ref[...] = jnp.zeros_like(s_ref)

    consts_f = _scan_consts(False)
    consts_b = _scan_consts(True)

    def body(j, carry):
        items, dests = [], []
        for sub in range(SCAN_UNROLL):
            for reverse, refs, y_ref, consts in ((False, (rf, vf, kkf, ewf, kf, bf), yf_ref, consts_f),
                                                 (True, (rb, vb, kkb, ewb, kb, bb), yb_ref, consts_b)):
                step = j * SCAN_UNROLL + sub
                cj = (nch - 1 - step) if reverse else step
                rows = pl.ds(pl.multiple_of(cj * CHUNK, CHUNK), CHUNK)
                for g in range(N_GROUPS):
                    lanes = slice(g * GROUP_LANES, (g + 1) * GROUP_LANES)
                    items.append(tuple(ref[0, rows, lanes].astype(f32) for ref in refs) + (consts, reverse))
                    dests.append((y_ref, rows, lanes, (2 if reverse else 0) + g))
        Q, Y0, Pm, Qm, D = _wkv_chunks(items)
        S = [s_ref[si] for si in range(2 * N_GROUPS)]
        for i, (y_ref, rows, lanes, si) in enumerate(dests):
            y_ref[0, rows, lanes] = _mm(Q[i], S[si], NT) + Y0[i]
            S[si] = S[si] * D[i] + _mm(S[si], Pm[i]) + Qm[i]
        for si in range(2 * N_GROUPS):
            s_ref[si] = S[si]
        return carry

    lax.fori_loop(0, nch // SCAN_UNROLL, body, 0)


def _scan(r, v, kk, ew0, ew1, k0, k1, b0, b1):
    b, t, _ = r.shape
    tb = _tile(t, SCAN_BLOCK)
    assert tb % (CHUNK * SCAN_UNROLL) == 0
    nb = t // tb
    fwd = pl.BlockSpec((1, tb, D_RWKV), lambda bi, ti: (bi, ti, 0))
    bwd = pl.BlockSpec((1, tb, D_RWKV), lambda bi, ti: (bi, nb - 1 - ti, 0))
    o_shape = jax.ShapeDtypeStruct((b, t, D_RWKV), f32)
    return pl.pallas_call(
        _scan_kernel,
        grid=(b, nb),
        in_specs=[fwd] * 6 + [bwd] * 6,
        out_specs=[fwd, bwd],
        out_shape=[o_shape, o_shape],
        scratch_shapes=[pltpu.VMEM((2 * N_GROUPS, GROUP_LANES, GROUP_LANES), f32)],
        compiler_params=_cparams(("parallel", "arbitrary")),
        name="wkv_scan",
    )(r, v, kk, ew0, k0, b0, r, v, kk, ew1, k1, b1)


CONV_HALO = 16


def _conv_kernel(uc_ref, up_ref, un_ref, kern_ref, bias_ref, lnw_ref, lnb_ref, o_ref, ext_ref, sh_ref):
    tt = uc_ref.shape[1]
    i = pl.program_id(1)
    nt = pl.num_programs(1)

    def glu(u):
        u = u.astype(f32)
        return u[:, :C_CONV] * _sigmoid(u[:, C_CONV:])

    ext_ref[0:CONV_HALO, :] = jnp.where(i == 0, 0.0, glu(up_ref[0]))
    ext_ref[CONV_HALO:CONV_HALO + tt, :] = glu(uc_ref[0])
    ext_ref[CONV_HALO + tt:, :] = jnp.where(i == nt - 1, 0.0, glu(un_ref[0]))
    span = sh_ref.shape[1]
    for r in range(1, SUBLANES_V7X):
        sh_ref[r] = ext_ref[r:r + span, :]
    acc = jnp.zeros((tt, C_CONV), f32) + bias_ref[...]
    half = CONV_W // 2
    for j in range(CONV_W):
        off = CONV_HALO - half + j
        r, base = off % SUBLANES_V7X, off - off % SUBLANES_V7X
        tap = ext_ref[base:base + tt, :] if r == 0 else sh_ref[r, base:base + tt, :]
        acc = acc + kern_ref[j:j + 1, :] * tap
    mu = jnp.mean(acc, axis=-1, keepdims=True)
    xc = acc - mu
    var = jnp.mean(xc * xc, axis=-1, keepdims=True)
    y = xc * lax.rsqrt(var + LN_EPS) * lnw_ref[...] + lnb_ref[...]
    o_ref[0] = (y * _sigmoid(y)).astype(bf16)


def _conv(u3, kern, bias, lnw, lnb):
    b, t, _ = u3.shape
    tt = _tile(t, 512)
    hb = tt // CONV_HALO
    full = lambda s: pl.BlockSpec(s, lambda bi, ti: tuple(0 for _ in s))
    return pl.pallas_call(
        _conv_kernel,
        grid=(b, t // tt),
        in_specs=[
            pl.BlockSpec((1, tt, 2 * C_CONV), lambda bi, ti: (bi, ti, 0)),
            pl.BlockSpec((1, CONV_HALO, 2 * C_CONV), lambda bi, ti: (bi, jnp.maximum(ti * hb - 1, 0), 0)),
            pl.BlockSpec((1, CONV_HALO, 2 * C_CONV),
                         lambda bi, ti: (bi, jnp.minimum((ti + 1) * hb, t // CONV_HALO - 1), 0)),
            full((CONV_W, C_CONV)), full((1, C_CONV)), full((1, C_CONV)), full((1, C_CONV)),
        ],
        out_specs=pl.BlockSpec((1, tt, C_CONV), lambda bi, ti: (bi, ti, 0)),
        out_shape=jax.ShapeDtypeStruct((b, t, C_CONV), bf16),
        scratch_shapes=[pltpu.VMEM((tt + 2 * CONV_HALO, C_CONV), f32),
                        pltpu.VMEM((SUBLANES_V7X, tt + 2 * CONV_HALO - SUBLANES_V7X, C_CONV), f32)],
        compiler_params=_cparams(("parallel", "parallel")),
        name="conv_branch",
    )(u3, u3, u3, kern, bias, lnw, lnb)


def _merge_kernel(x_ref, yf_ref, yb_ref, g_ref, bonus_ref, cb_ref, nmix_ref, wg_ref, lnxw_ref, lnxb_ref,
                  havg_ref, wor_ref, woc_ref, wout_ref, nffn_ref, rwt_ref, rb_ref,
                  x1_ref, h2_ref, aff_ref):
    x = x_ref[...]
    h = _rms(x, nmix_ref[...]).astype(bf16)
    gates = _sigmoid(jnp.dot(h, wg_ref[...], preferred_element_type=f32))
    wkv = yf_ref[...] + yb_ref[...]
    havg = havg_ref[...]
    mu = _mm_split_lhs(wkv, havg, 2)
    xc = wkv - mu
    var = _mm_split_lhs(xc * xc, havg, 2)
    y = (xc * lax.rsqrt(var + GN_EPS) * lnxw_ref[...] + lnxb_ref[...] + bonus_ref[...]) * g_ref[...]
    y_a = jnp.dot(y.astype(bf16), wor_ref[...], preferred_element_type=f32)
    y_b = jnp.dot(cb_ref[...].astype(bf16), woc_ref[...], preferred_element_type=f32)
    m = gates[:, :D_MODEL] * y_a + gates[:, D_MODEL:] * y_b
    x1 = x + jnp.dot(m.astype(bf16), wout_ref[...], preferred_element_type=f32)
    x1_ref[...] = x1
    h2 = _rms(x1, nffn_ref[...])
    h2_ref[...] = h2
    logits = lax.dot_general(rwt_ref[...], h2.astype(bf16), NT, preferred_element_type=f32) + rb_ref[...]
    mx = jnp.max(logits, axis=0, keepdims=True)
    e = jnp.exp(logits - mx)
    aff_ref[...] = e / jnp.sum(e, axis=0, keepdims=True)


def _merge(x2, yf, yb, g, bonus, cb, nmix, wg, lnxw, lnxb, havg, wor, woc, wout, nffn, rwt, rb):
    n = x2.shape[0]
    tm = _tile(n, 512)
    row = lambda c: pl.BlockSpec((tm, c), lambda i: (i, 0))
    full = lambda s: pl.BlockSpec(s, lambda i: tuple(0 for _ in s))
    return pl.pallas_call(
        _merge_kernel,
        grid=(n // tm,),
        in_specs=[
            row(D_MODEL), row(D_RWKV), row(D_RWKV), row(D_RWKV), row(D_RWKV), row(C_CONV),
            full((1, D_MODEL)), full((D_MODEL, 2 * D_MODEL)), full((1, D_RWKV)), full((1, D_RWKV)),
            full((D_RWKV, D_RWKV)), full((D_RWKV, D_MODEL)), full((C_CONV, D_MODEL)),
            full((D_MODEL, D_MODEL)), full((1, D_MODEL)), full((N_EXPERTS, D_MODEL)), full((N_EXPERTS, 1)),
        ],
        out_specs=[row(D_MODEL), row(D_MODEL), pl.BlockSpec((N_EXPERTS, tm), lambda i: (0, i))],
        out_shape=[
            jax.ShapeDtypeStruct((n, D_MODEL), f32),
            jax.ShapeDtypeStruct((n, D_MODEL), f32),
            jax.ShapeDtypeStruct((N_EXPERTS, n), f32),
        ],
        compiler_params=_cparams(("parallel",)),
        name="merge_router",
    )(x2, yf, yb, g, bonus, cb, nmix, wg, lnxw, lnxb, havg, wor, woc, wout, nffn, rwt, rb)


def _thr_kernel(aff_ref, thr_ref, *, cap):
    def count_ge(mid):
        bits = pltpu.bitcast(aff_ref[...], i32)
        return jnp.sum(jnp.where(bits >= mid, 1.0, 0.0), axis=1, keepdims=True)

    def body(_, c):
        lo, hi = c
        mid = lo + ((hi - lo + 1) >> 1)
        ok = count_ge(mid) >= cap
        return jnp.where(ok, mid, lo), jnp.where(ok, hi, mid - 1)

    lo0 = jnp.zeros((N_EXPERTS, 1), i32)
    hi0 = jnp.full((N_EXPERTS, 1), 0x3F800000, i32)
    lo, _ = lax.fori_loop(0, 31, body, (lo0, hi0))
    thr_ref[...] = jnp.broadcast_to(lo, thr_ref.shape)


def _threshold(aff_t, cap):
    e, n = aff_t.shape
    return pl.pallas_call(
        functools.partial(_thr_kernel, cap=float(cap)),
        in_specs=[pl.BlockSpec((e, n), lambda: (0, 0))],
        out_specs=pl.BlockSpec((e, LANES_V7X), lambda: (0, 0)),
        out_shape=jax.ShapeDtypeStruct((e, LANES_V7X), i32),
        compiler_params=pltpu.CompilerParams(vmem_limit_bytes=VMEM_LIMIT_V7X),
        name="route_threshold",
    )(aff_t)


def _select(aff_t, cap):
    thr = _threshold(aff_t, cap)[:, :1]
    bits = lax.bitcast_convert_type(aff_t, i32)
    gt = bits > thr
    eq = bits == thr
    need = cap - jnp.sum(gt, axis=1, keepdims=True)
    sel = gt | (eq & (jnp.cumsum(eq, axis=1) <= need))
    cum = jnp.cumsum(sel, axis=1)
    slots = jnp.arange(1, cap + 1, dtype=cum.dtype)
    idx = jax.vmap(lambda c: jnp.searchsorted(c, slots, side="left"))(cum).astype(i32)
    gate = jnp.take_along_axis(aff_t, idx, axis=1)
    return idx, gate, sel, cum.astype(i32)


def _ffn_kernel(idx_ref, idxn_ref, gate_ref, h2_hbm, wg_ref, wu_ref, wd_ref, ye_ref, xbuf, sem):
    tc = xbuf.shape[1]
    s = pl.program_id(0)
    last = pl.num_programs(0) - 1
    slot = s % 2

    def x_copy(ref, i, sl):
        return pltpu.make_async_copy(h2_hbm.at[pl.ds(ref[0, 0, i], 1)], xbuf.at[sl, pl.ds(i, 1)], sem.at[sl])

    def wait_tile(sl):
        pltpu.make_async_copy(h2_hbm.at[pl.ds(0, tc)], xbuf.at[sl], sem.at[sl]).wait()

    @pl.when(s == 0)
    def _():
        def body(i, c):
            x_copy(idx_ref, i, 0).start()
            return c
        lax.fori_loop(0, tc, body, 0, unroll=8)

    wait_tile(slot)
    xe = xbuf[slot].astype(bf16)
    hg = jnp.dot(xe, wg_ref[0], preferred_element_type=f32)
    for i in range(tc // 2):
        x_copy(idxn_ref, i, 1 - slot).start()
    hu = jnp.dot(xe, wu_ref[0], preferred_element_type=f32)
    hid = (hg * _sigmoid(hg) * hu).astype(bf16)
    for i in range(tc // 2, tc):
        x_copy(idxn_ref, i, 1 - slot).start()
    ye_ref[...] = jnp.dot(hid, wd_ref[0], preferred_element_type=f32) * gate_ref[0]

    @pl.when(s == last)
    def _():
        wait_tile(1 - slot)


def _ffn(idx, gate, h2, wg, wu, wd):
    e, cap = idx.shape
    tc = _tile(cap, 256)
    nt = cap // tc
    steps = e * nt
    wspec = pl.BlockSpec((1, D_MODEL, D_FF), lambda si: (si // nt, 0, 0))
    idx3 = idx.reshape(steps, 1, tc)
    return pl.pallas_call(
        _ffn_kernel,
        grid=(steps,),
        in_specs=[
            pl.BlockSpec((1, 1, tc), lambda si: (si, 0, 0), memory_space=pltpu.SMEM),
            pl.BlockSpec((1, 1, tc), lambda si: (jnp.minimum(si + 1, steps - 1), 0, 0), memory_space=pltpu.SMEM),
            pl.BlockSpec((1, tc, 1), lambda si: (si, 0, 0)),
            pl.BlockSpec(memory_space=pl.ANY),
            wspec, wspec,
            pl.BlockSpec((1, D_FF, D_MODEL), lambda si: (si // nt, 0, 0)),
        ],
        out_specs=pl.BlockSpec((tc, D_MODEL), lambda si: (si, 0)),
        out_shape=jax.ShapeDtypeStruct((e * cap, D_MODEL), f32),
        scratch_shapes=[pltpu.VMEM((2, tc, D_MODEL), f32), pltpu.SemaphoreType.DMA((2,))],
        compiler_params=_cparams(("arbitrary",)),
        name="expert_ffn",
    )(idx3, idx3, gate.reshape(steps, tc, 1), h2, wg, wu, wd)


COMB_CH = 32
COMB_KB = 256
META_START, META_NCH, META_OFF, META_SLOT, META_TOT = 0, 16, 32, 48, 64
META_W = 128
NOT_ROUTED = -(1 << 30)


def _combine_kernel(meta_ref, x1_ref, pos_ref, ye_hbm, g_ref, o_ref, slab, oht, acc, sem):
    tn = x1_ref.shape[0]
    i = pl.program_id(0)

    @pl.when(i == 0)
    def _():
        slab[...] = jnp.zeros_like(slab)

    def piece(e, c):
        src = ye_hbm.at[pl.ds(pl.multiple_of(meta_ref[0, 0, META_START + e] + c * COMB_CH, SUBLANES_V7X), COMB_CH)]
        dst = slab.at[pl.ds(pl.multiple_of(meta_ref[0, 0, META_OFF + e] + c * COMB_CH, COMB_CH), COMB_CH)]
        return pltpu.make_async_copy(src, dst, sem.at[0])

    for e in range(N_EXPERTS):
        def start(c, carry, e=e):
            piece(e, c).start()
            return carry
        lax.fori_loop(0, meta_ref[0, 0, META_NCH + e], start, 0)

    rowi = lax.broadcasted_iota(i32, (COMB_CH, 1), 0)
    for e in range(N_EXPERTS):
        pos_e = pos_ref[e:e + 1, :]

        def fill(c, carry, e=e, pos_e=pos_e):
            slot = meta_ref[0, 0, META_SLOT + e] + c * COMB_CH + rowi
            row0 = pl.multiple_of(meta_ref[0, 0, META_OFF + e] + c * COMB_CH, COMB_CH)
            oht[pl.ds(row0, COMB_CH), :] = jnp.where(pos_e == slot, 1.0, 0.0).astype(bf16)
            return carry
        lax.fori_loop(0, meta_ref[0, 0, META_NCH + e], fill, 0)

    ktot = meta_ref[0, 0, META_TOT]
    nkb = (ktot + COMB_KB - 1) // COMB_KB

    def clear(c, carry):
        row0 = pl.multiple_of(ktot + c * COMB_CH, COMB_CH)
        oht[pl.ds(row0, COMB_CH), :] = jnp.zeros((COMB_CH, tn), bf16)
        return carry
    lax.fori_loop(0, (nkb * COMB_KB - ktot) // COMB_CH, clear, 0)

    for e in range(N_EXPERTS):
        def wait(c, carry, e=e):
            piece(e, c).wait()
            return carry
        lax.fori_loop(0, meta_ref[0, 0, META_NCH + e], wait, 0)

    acc[...] = x1_ref[...]

    def mm(j, carry):
        rows = pl.ds(pl.multiple_of(j * COMB_KB, COMB_KB), COMB_KB)
        acc[...] += lax.dot_general(oht[rows, :], slab[rows, :].astype(bf16), TN, preferred_element_type=f32)
        return carry
    lax.fori_loop(0, nkb, mm, 0)
    o_ref[...] = _rms(acc[...], g_ref[...])


def _combine(meta, x1, pos, ye, g):
    n = x1.shape[0]
    tn = _tile(n, 256)
    kmax = N_EXPERTS * (tn + 2 * COMB_CH)
    assert kmax % COMB_KB == 0
    return pl.pallas_call(
        _combine_kernel,
        grid=(n // tn,),
        in_specs=[
            pl.BlockSpec((1, 1, META_W), lambda i: (i, 0, 0), memory_space=pltpu.SMEM),
            pl.BlockSpec((tn, D_MODEL), lambda i: (i, 0)),
            pl.BlockSpec((N_EXPERTS, tn), lambda i: (0, i)),
            pl.BlockSpec(memory_space=pl.ANY),
            pl.BlockSpec((1, D_MODEL), lambda i: (0, 0)),
        ],
        out_specs=pl.BlockSpec((tn, D_MODEL), lambda i: (i, 0)),
        out_shape=jax.ShapeDtypeStruct((n, D_MODEL), f32),
        scratch_shapes=[pltpu.VMEM((kmax, D_MODEL), f32), pltpu.VMEM((kmax, tn), bf16),
                        pltpu.VMEM((tn, D_MODEL), f32), pltpu.SemaphoreType.DMA((1,))],
        compiler_params=_cparams(("arbitrary",)),
        name="combine_norm",
    )(meta, x1, pos, ye, g)


def _combine_meta(sel, cum, cap, tn):
    e, n = cum.shape
    nt = n // tn
    end = cum[:, tn - 1::tn]
    before = jnp.concatenate([jnp.zeros((e, 1), i32), end[:, :-1]], axis=1)
    base = jnp.arange(e, dtype=i32)[:, None] * cap
    start = (base + before) // SUBLANES_V7X * SUBLANES_V7X
    nch = jnp.where(end > before, (base + end - start + COMB_CH - 1) // COMB_CH, 0)
    start = jnp.minimum(start, e * cap - nch * COMB_CH)
    rows = nch * COMB_CH
    off = jnp.cumsum(rows, axis=0) - rows
    tot = jnp.sum(rows, axis=0, keepdims=True)
    pad = jnp.zeros((META_W - META_TOT - 1, nt), i32)
    meta = jnp.concatenate([start, nch, off, start - base, tot, pad], axis=0)
    pos = jnp.where(sel, cum - 1, NOT_ROUTED).astype(i32)
    return meta.T.reshape(nt, 1, META_W), pos


def _prepare_params(norm_mix, w_in, shift_mu_prev, shift_mu_next, decay_w0, decay_up, iclr_a0, iclr_up,
                    gate_up, k_k, k_a, r_k, lnx_w, lnx_b, w_o_rwkv, dw_kernel, dw_bias, conv_ln_w, conv_ln_b,
                    w_o_conv, w_out, norm_ffn, router_w, router_b, exp_w_gate, exp_w_up, exp_w_down, norm_final):
    l = 0
    rwkv_cols = RKV_COLS + LORA_COLS
    w = w_in[l]
    pad = jnp.zeros((D_MODEL, LORA_PAD - LORA_COLS), f32)
    wa = jnp.concatenate([w[:, :rwkv_cols], pad], axis=1).astype(bf16)
    wu = w[:, rwkv_cols:rwkv_cols + 2 * C_CONV].astype(bf16)
    wgate = w[:, rwkv_cols + 2 * C_CONV:].astype(bf16)
    padv = jnp.zeros((LORA_PAD - LORA_COLS,), f32)
    mup = jnp.concatenate([shift_mu_prev[l], padv])[None, :]
    mun = jnp.concatenate([shift_mu_next[l], padv])[None, :]
    wup = jnp.zeros((LORA_PAD, UP_COLS), f32)
    wup = wup.at[0:W_LORA, 0:D_RWKV].set(decay_up[l, 0])
    wup = wup.at[W_LORA:2 * W_LORA, D_RWKV:2 * D_RWKV].set(decay_up[l, 1])
    o = 2 * W_LORA
    wup = wup.at[o:o + A_LORA, 2 * D_RWKV:3 * D_RWKV].set(iclr_up[l, 0])
    wup = wup.at[o + A_LORA:o + 2 * A_LORA, 3 * D_RWKV:4 * D_RWKV].set(iclr_up[l, 1])
    o = 2 * W_LORA + 2 * A_LORA
    wup = wup.at[o:o + G_LORA, 4 * D_RWKV:5 * D_RWKV].set(gate_up[l])
    head = jnp.arange(D_RWKV) // HEAD
    same = (head[:, None] == head[None, :])
    return dict(
        norm_mix=norm_mix[l][None, :], wa=wa, wu=wu, wgate=wgate, mup=mup, mun=mun, wup=wup.astype(bf16),
        w0=decay_w0[l], a0=iclr_a0[l], kkw=k_k[l][None, :], ka=k_a[l][None, :], rk=r_k[l].reshape(1, D_RWKV),
        hsum=same.astype(bf16), havg=(same.astype(f32) / HEAD).astype(bf16),
        lnxw=lnx_w[l][None, :], lnxb=lnx_b[l][None, :], wor=w_o_rwkv[l].astype(bf16),
        kern=dw_kernel[l], cbias=dw_bias[l][None, :], clnw=conv_ln_w[l][None, :], clnb=conv_ln_b[l][None, :],
        woc=w_o_conv[l].astype(bf16), wout=w_out[l].astype(bf16), nffn=norm_ffn[l][None, :],
        rwt=router_w[l].T.astype(bf16), rb=router_b[l][:, None],
        ewg=exp_w_gate[l].astype(bf16), ewu=exp_w_up[l].astype(bf16), ewd=exp_w_down[l].astype(bf16),
        nfinal=norm_final[None, :],
    )


def _trunk(x, P):
    b, t, d = x.shape
    n = b * t
    x2 = x.reshape(n, d)
    p, u = _inproj(x2, P["norm_mix"], P["wa"], P["wu"])
    r, v, kk, ew0, ew1, k0, k1, b0, b1, g, bonus = _prep(
        p.reshape(b, t, PA_COLS), P["mup"], P["mun"], P["wup"], P["w0"], P["a0"], P["kkw"], P["ka"], P["rk"],
        P["hsum"])
    yf, yb = _scan(r, v, kk, ew0, ew1, k0, k1, b0, b1)
    cb = _conv(u.reshape(b, t, 2 * C_CONV), P["kern"], P["cbias"], P["clnw"], P["clnb"])
    flat = lambda a: a.reshape(n, a.shape[-1])
    x1, h2, aff_t = _merge(x2, flat(yf), flat(yb), flat(g), flat(bonus), flat(cb), P["norm_mix"], P["wgate"],
                           P["lnxw"], P["lnxb"], P["havg"], P["wor"], P["woc"], P["wout"], P["nffn"],
                           P["rwt"], P["rb"])
    cap = max(1, CAPACITY_FACTOR * n // N_EXPERTS)
    idx, gate, sel, cum = _select(aff_t, cap)
    ye = _ffn(idx, gate, h2, P["ewg"], P["ewu"], P["ewd"])
    meta, pos = _combine_meta(sel, cum, cap, _tile(n, 256))
    return _combine(meta, x1, pos, ye, P["nfinal"]).reshape(b, t, d)


def kernel(x_prompt, x_sample, norm_mix, w_in, shift_mu_prev, shift_mu_next, decay_w0, decay_up, iclr_a0, iclr_up, gate_up, k_k, k_a, r_k, lnx_w, lnx_b, w_o_rwkv, dw_kernel, dw_bias, conv_ln_w, conv_ln_b, w_o_conv, w_out, norm_ffn, router_w, router_b, exp_w_gate, exp_w_up, exp_w_down, norm_final):
    P = _prepare_params(norm_mix, w_in, shift_mu_prev, shift_mu_next, decay_w0, decay_up, iclr_a0, iclr_up,
                        gate_up, k_k, k_a, r_k, lnx_w, lnx_b, w_o_rwkv, dw_kernel, dw_bias, conv_ln_w,
                        conv_ln_b, w_o_conv, w_out, norm_ffn, router_w, router_b, exp_w_gate, exp_w_up,
                        exp_w_down, norm_final)
    return (_trunk(x_prompt, P), _trunk(x_sample, P))
```

```python
import functools
import math

import jax
import jax.numpy as jnp
from jax import lax
from jax.experimental import pallas as pl
from jax.experimental.pallas import tpu as pltpu

f32 = jnp.float32
bf16 = jnp.bfloat16
i32 = jnp.int32

D_MODEL = 1024
D_RWKV = 512
HEAD = 64
W_LORA = 32
A_LORA = 32
G_LORA = 96
C_CONV = 512
CONV_W = 31
N_EXPERTS = 16
D_FF = 1024
CAPACITY_FACTOR = 2
GN_EPS = 64e-5
LN_EPS = 1e-5
RMS_EPS = 1e-6
LORA_COLS = 2 * W_LORA + 2 * A_LORA + G_LORA
LORA_PAD = 256
RKV_COLS = 3 * D_RWKV
PA_COLS = RKV_COLS + LORA_PAD
UP_COLS = 5 * D_RWKV

LANES_V7X = 128
SUBLANES_V7X = 8
MXU_DIM_V7X = 256
VMEM_LIMIT_V7X = 56 * 1024 * 1024

CHUNK = 64
SCAN_UNROLL = 2
SCAN_BLOCK = 512
GROUP = MXU_DIM_V7X // HEAD
GROUP_LANES = GROUP * HEAD
N_GROUPS = D_RWKV // GROUP_LANES

NN = (((1,), (0,)), ((), ()))
NT = (((1,), (1,)), ((), ()))
TN = (((0,), (0,)), ((), ()))


def _mm(a, b, dims=NN):
    return lax.dot_general(a.astype(bf16), b.astype(bf16), dims, preferred_element_type=f32)


def _mm_split(a, x, n):
    out = None
    rem = x
    for _ in range(n):
        hi = rem.astype(bf16)
        t = lax.dot_general(a, hi, NN, preferred_element_type=f32)
        out = t if out is None else out + t
        rem = rem - hi.astype(f32)
    return out


def _mm_split_lhs(x, a, n):
    out = None
    rem = x
    for _ in range(n):
        hi = rem.astype(bf16)
        t = lax.dot_general(hi, a, NN, preferred_element_type=f32)
        out = t if out is None else out + t
        rem = rem - hi.astype(f32)
    return out


def _sigmoid(x):
    return 1.0 / (1.0 + jnp.exp(-x))


def _rms(x, g):
    return x * lax.rsqrt(jnp.mean(x * x, axis=-1, keepdims=True) + RMS_EPS) * g


def _cparams(sem):
    return pltpu.CompilerParams(dimension_semantics=sem, vmem_limit_bytes=VMEM_LIMIT_V7X)


def _tile(n, pref):
    t = min(n, pref)
    assert n % t == 0, (n, pref)
    return t


def _inproj_kernel(x_ref, g_ref, wa_ref, wu_ref, p_ref, u_ref):
    h = _rms(x_ref[...], g_ref[...]).astype(bf16)
    p_ref[...] = jnp.dot(h, wa_ref[...], preferred_element_type=f32).astype(bf16)
    u_ref[...] = jnp.dot(h, wu_ref[...], preferred_element_type=f32).astype(bf16)


def _inproj(x2, g, wa, wu):
    n = x2.shape[0]
    tm = _tile(n, 512)
    return pl.pallas_call(
        _inproj_kernel,
        grid=(n // tm,),
        in_specs=[
            pl.BlockSpec((tm, D_MODEL), lambda i: (i, 0)),
            pl.BlockSpec((1, D_MODEL), lambda i: (0, 0)),
            pl.BlockSpec((D_MODEL, PA_COLS), lambda i: (0, 0)),
            pl.BlockSpec((D_MODEL, 2 * C_CONV), lambda i: (0, 0)),
        ],
        out_specs=[
            pl.BlockSpec((tm, PA_COLS), lambda i: (i, 0)),
            pl.BlockSpec((tm, 2 * C_CONV), lambda i: (i, 0)),
        ],
        out_shape=[
            jax.ShapeDtypeStruct((n, PA_COLS), bf16),
            jax.ShapeDtypeStruct((n, 2 * C_CONV), bf16),
        ],
        compiler_params=_cparams(("parallel",)),
        name="inproj",
    )(x2, g, wa, wu)


PREP_HALO = 16


def _prep_kernel(pc_ref, pp_ref, pn_ref, mup_ref, mun_ref, wup_ref, w0_ref, a0_ref, kkw_ref, ka_ref,
                 rk_ref, hsum_ref, r_o, v_o, kk_o, ew0_o, ew1_o, k0_o, k1_o, b0_o, b1_o, g_o, bonus_o):
    tt = pc_ref.shape[1]
    i = pl.program_id(1)
    nt = pl.num_programs(1)
    row = lax.broadcasted_iota(i32, (tt, 1), 0)
    first = i == 0
    last = i == nt - 1

    def shifted(c0, c1):
        p = pc_ref[0, :, c0:c1].astype(f32)
        prev_row = jnp.where(first, 0.0, pp_ref[0, PREP_HALO - 1:PREP_HALO, c0:c1].astype(f32))
        next_row = jnp.where(last, 0.0, pn_ref[0, 0:1, c0:c1].astype(f32))
        zp = jnp.where(row == 0, prev_row, pltpu.roll(p, 1, axis=0))
        zn = jnp.where(row == tt - 1, next_row, pltpu.roll(p, tt - 1, axis=0))
        return p + mup_ref[:, c0:c1] * (zp - p) + mun_ref[:, c0:c1] * (zn - p)

    r = shifted(0, D_RWKV)
    k = shifted(D_RWKV, 2 * D_RWKV)
    v = shifted(2 * D_RWKV, 3 * D_RWKV)
    zl = shifted(RKV_COLS, PA_COLS)

    lane = lax.broadcasted_iota(i32, (1, LORA_PAD), 1)
    act = jnp.where(lane < 2 * W_LORA, jnp.tanh(zl),
                    jnp.where(lane < 2 * W_LORA + 2 * A_LORA, zl, _sigmoid(zl)))
    pre = jnp.dot(act.astype(bf16), wup_ref[...], preferred_element_type=f32)

    hsum = hsum_ref[...]
    kkr = k * kkw_ref[...]
    ss = _mm_split_lhs(kkr * kkr, hsum, 2)
    kk = kkr / jnp.maximum(jnp.sqrt(ss), 1e-12)
    ka = ka_ref[...]
    ksum = None
    outs = ((ew0_o, k0_o, b0_o), (ew1_o, k1_o, b1_o))
    for d in range(2):
        x = pre[:, d * D_RWKV:(d + 1) * D_RWKV] + w0_ref[d:d + 1, :]
        ew = math.exp(-0.5) * _sigmoid(x)
        alpha = _sigmoid(pre[:, (2 + d) * D_RWKV:(3 + d) * D_RWKV] + a0_ref[d:d + 1, :])
        kd = k * (1.0 + (alpha - 1.0) * ka)
        ew_o, k_o, b_o = outs[d]
        ew_o[0] = ew
        k_o[0] = kd.astype(bf16)
        b_o[0] = (kk * alpha).astype(bf16)
        ksum = kd if ksum is None else ksum + kd
    bonus = _mm_split_lhs(r * ksum * rk_ref[...], hsum, 2) * v
    r_o[0] = r.astype(bf16)
    v_o[0] = v.astype(bf16)
    kk_o[0] = kk.astype(bf16)
    g_o[0] = pre[:, 4 * D_RWKV:5 * D_RWKV].astype(bf16)
    bonus_o[0] = bonus.astype(bf16)


def _prep(p3, mup, mun, wup, w0, a0, kkw, ka, rk, hsum):
    b, t, _ = p3.shape
    tt = _tile(t, 256)
    nt = t // tt
    hb = tt // PREP_HALO
    full = lambda s: pl.BlockSpec(s, lambda bi, ti: tuple(0 for _ in s))
    o_spec = pl.BlockSpec((1, tt, D_RWKV), lambda bi, ti: (bi, ti, 0))
    o_lo = jax.ShapeDtypeStruct((b, t, D_RWKV), bf16)
    o_hi = jax.ShapeDtypeStruct((b, t, D_RWKV), f32)
    return pl.pallas_call(
        _prep_kernel,
        grid=(b, nt),
        in_specs=[
            pl.BlockSpec((1, tt, PA_COLS), lambda bi, ti: (bi, ti, 0)),
            pl.BlockSpec((1, PREP_HALO, PA_COLS), lambda bi, ti: (bi, jnp.maximum(ti * hb - 1, 0), 0)),
            pl.BlockSpec((1, PREP_HALO, PA_COLS),
                         lambda bi, ti: (bi, jnp.minimum((ti + 1) * hb, t // PREP_HALO - 1), 0)),
            full((1, PA_COLS)), full((1, PA_COLS)), full((LORA_PAD, UP_COLS)),
            full((2, D_RWKV)), full((2, D_RWKV)), full((1, D_RWKV)), full((1, D_RWKV)), full((1, D_RWKV)),
            full((D_RWKV, D_RWKV)),
        ],
        out_specs=[o_spec] * 11,
        out_shape=[o_lo, o_lo, o_lo, o_hi, o_hi] + [o_lo] * 6,
        compiler_params=_cparams(("parallel", "parallel")),
        name="rwkv_prep",
    )(p3, p3, p3, mup, mun, wup, w0, a0, kkw, ka, rk, hsum)


def _scan_consts(reverse):
    L = CHUNK
    t = lax.broadcasted_iota(i32, (L, GROUP * L), 0)
    s = lax.broadcasted_iota(i32, (L, GROUP * L), 1) & (L - 1)
    strict = (s > t) if reverse else (s < t)
    eye = s == t
    incl = strict | eye
    tt = lax.broadcasted_iota(i32, (L, L), 0)
    ss = lax.broadcasted_iota(i32, (L, L), 1)
    tri = jnp.where((ss >= tt) if reverse else (ss <= tt), 1.0, 0.0).astype(bf16)
    rr = lax.broadcasted_iota(i32, (GROUP_LANES, GROUP_LANES), 0) // HEAD
    cc = lax.broadcasted_iota(i32, (GROUP_LANES, GROUP_LANES), 1) // HEAD
    same = rr == cc
    return strict, incl, eye, tri, (same, jnp.where(same, 1.0, 0.0).astype(bf16))


def _bd(x, bdmask):
    xb = x.astype(bf16)
    return jnp.concatenate([xb] * GROUP, axis=0) * bdmask[1]


def _wkv_chunks(chains):
    L = CHUNK
    n = len(chains)
    rng = range(n)
    r, v, kk, ew, k, b, consts, rev = (list(c) for c in zip(*chains))
    strict = [c[0] for c in consts]
    incl = [c[1] for c in consts]
    eye = [c[2] for c in consts]
    tri = [c[3] for c in consts]
    bdm = consts[0][4]
    ginc = [-_mm_split(tri[i], ew[i], 3) for i in rng]
    gref = [0.5 * (ginc[i][0:1] if rev[i] else ginc[i][L - 1:L]) for i in rng]
    egref = [jnp.exp(gref[i]) for i in rng]
    e1 = [jnp.exp(ginc[i] - gref[i]) for i in rng]
    e2 = [jnp.exp(gref[i] - ginc[i]) for i in rng]
    ea = [jnp.exp(ginc[i] + ew[i] - gref[i]) for i in rng]
    At = [-kk[i] * ea[i] for i in rng]
    Rt = [r[i] * e1[i] for i in rng]
    Kt = [k[i] * e2[i] for i in rng]
    Bt = [b[i] * e2[i] for i in rng]
    Be = [Bt[i] * egref[i] for i in rng]
    AR = [jnp.concatenate([At[i], Rt[i]], axis=0) for i in rng]
    pb = [_mm(AR[i], _bd(Bt[i], bdm), NT) for i in rng]
    pk = [_mm(AR[i], _bd(Kt[i], bdm), NT) for i in rng]
    Mab = [jnp.where(strict[i], pb[i][:L], 0.0) for i in rng]
    Arb = [jnp.where(incl[i], pb[i][L:], 0.0) for i in rng]
    Mak = [jnp.where(strict[i], pk[i][:L], 0.0) for i in rng]
    Ark = [jnp.where(incl[i], pk[i][L:], 0.0) for i in rng]
    vbd = [_bd(v[i], bdm) for i in rng]
    MakV = [_mm(Mak[i], vbd[i]) for i in rng]
    X = [jnp.where(eye[i], 1.0, Mab[i]) for i in rng]
    P = [_mm(Mab[i], _bd(Mab[i], bdm)) for i in rng]
    for step in range(4):
        XP = [_mm(jnp.concatenate([X[i], P[i]], axis=0), _bd(P[i], bdm)) for i in rng]
        X = [X[i] + XP[i][:L] for i in rng]
        P = [XP[i][L:] for i in rng]
    X = [X[i] + _mm(X[i], _bd(P[i], bdm)) for i in rng]
    W1 = [_mm(X[i], _bd(At[i] * egref[i], bdm)) for i in rng]
    W2 = [_mm(X[i], _bd(MakV[i], bdm)) for i in rng]
    Q = [Rt[i] * egref[i] + _mm(Arb[i], _bd(W1[i], bdm)) for i in rng]
    Y0 = [_mm(Arb[i], _bd(W2[i], bdm)) + _mm(Ark[i], vbd[i]) for i in rng]
    Pm = [jnp.where(bdm[0], _mm(W1[i], Be[i], TN), 0.0) for i in rng]
    Qm = [jnp.where(bdm[0], _mm(jnp.concatenate([W2[i], v[i]], axis=0),
                                jnp.concatenate([Be[i], Kt[i] * egref[i]], axis=0), TN), 0.0) for i in rng]
    D = [egref[i] * egref[i] for i in rng]
    return Q, Y0, Pm, Qm, D


def _scan_kernel(rf, vf, kkf, ewf, kf, bf, rb, vb, kkb, ewb, kb, bb, yf_ref, yb_ref, s_ref):
    tb = rf.shape[1]
    nch = tb // CHUNK

    @pl.when(pl.program_id(1) == 0)
    def _():
        s_ref[...] = jnp.zeros_like(s_ref)

    consts_f = _scan_consts(False)
    consts_b = _scan_consts(True)

    def body(j, carry):
        items, dests = [], []
        for sub in range(SCAN_UNROLL):
            for reverse, refs, y_ref, consts in ((False, (rf, vf, kkf, ewf, kf, bf), yf_ref, consts_f),
                                                 (True, (rb, vb, kkb, ewb, kb, bb), yb_ref, consts_b)):
                step = j * SCAN_UNROLL + sub
                cj = (nch - 1 - step) if reverse else step
                rows = pl.ds(pl.multiple_of(cj * CHUNK, CHUNK), CHUNK)
                for g in range(N_GROUPS):
                    lanes = slice(g * GROUP_LANES, (g + 1) * GROUP_LANES)
                    items.append(tuple(ref[0, rows, lanes].astype(f32) for ref in refs) + (consts, reverse))
                    dests.append((y_ref, rows, lanes, (2 if reverse else 0) + g))
        Q, Y0, Pm, Qm, D = _wkv_chunks(items)
        S = [s_ref[si] for si in range(2 * N_GROUPS)]
        for i, (y_ref, rows, lanes, si) in enumerate(dests):
            y_ref[0, rows, lanes] = _mm(Q[i], S[si], NT) + Y0[i]
            S[si] = S[si] * D[i] + _mm(S[si], Pm[i]) + Qm[i]
        for si in range(2 * N_GROUPS):
            s_ref[si] = S[si]
        return carry

    lax.fori_loop(0, nch // SCAN_UNROLL, body, 0)


def _scan(r, v, kk, ew0, ew1, k0, k1, b0, b1):
    b, t, _ = r.shape
    tb = _tile(t, SCAN_BLOCK)
    assert tb % (CHUNK * SCAN_UNROLL) == 0
    nb = t // tb
    fwd = pl.BlockSpec((1, tb, D_RWKV), lambda bi, ti: (bi, ti, 0))
    bwd = pl.BlockSpec((1, tb, D_RWKV), lambda bi, ti: (bi, nb - 1 - ti, 0))
    o_shape = jax.ShapeDtypeStruct((b, t, D_RWKV), f32)
    return pl.pallas_call(
        _scan_kernel,
        grid=(b, nb),
        in_specs=[fwd] * 6 + [bwd] * 6,
        out_specs=[fwd, bwd],
        out_shape=[o_shape, o_shape],
        scratch_shapes=[pltpu.VMEM((2 * N_GROUPS, GROUP_LANES, GROUP_LANES), f32)],
        compiler_params=_cparams(("parallel", "arbitrary")),
        name="wkv_scan",
    )(r, v, kk, ew0, k0, b0, r, v, kk, ew1, k1, b1)


CONV_HALO = 16


def _conv_tile(uc_ref, up_ref, un_ref, kern_ref, bias_ref, lnw_ref, lnb_ref, ext_ref, sh_ref, first, last):
    tt = uc_ref.shape[0]

    def glu(u):
        u = u.astype(f32)
        return u[:, :C_CONV] * _sigmoid(u[:, C_CONV:])

    ext_ref[0:CONV_HALO, :] = jnp.where(first, 0.0, glu(up_ref[...]))
    ext_ref[CONV_HALO:CONV_HALO + tt, :] = glu(uc_ref[...])
    ext_ref[CONV_HALO + tt:, :] = jnp.where(last, 0.0, glu(un_ref[...]))
    span = sh_ref.shape[1]
    for r in range(1, SUBLANES_V7X):
        sh_ref[r] = ext_ref[r:r + span, :]
    acc = jnp.zeros((tt, C_CONV), f32) + bias_ref[...]
    half = CONV_W // 2
    for j in range(CONV_W):
        off = CONV_HALO - half + j
        r, base = off % SUBLANES_V7X, off - off % SUBLANES_V7X
        tap = ext_ref[base:base + tt, :] if r == 0 else sh_ref[r, base:base + tt, :]
        acc = acc + kern_ref[j:j + 1, :] * tap
    mu = jnp.mean(acc, axis=-1, keepdims=True)
    xc = acc - mu
    var = jnp.mean(xc * xc, axis=-1, keepdims=True)
    y = xc * lax.rsqrt(var + LN_EPS) * lnw_ref[...] + lnb_ref[...]
    return y * _sigmoid(y)


def _merge_kernel(x_ref, yf_ref, yb_ref, g_ref, bonus_ref, uc_ref, up_ref, un_ref, kern_ref, cbias_ref,
                  clnw_ref, clnb_ref, nmix_ref, wg_ref, lnxw_ref, lnxb_ref,
                  havg_ref, wor_ref, woc_ref, wout_ref, nffn_ref, rwt_ref, rb_ref,
                  x1_ref, h2_ref, aff_ref, ext_ref, sh_ref, *, tiles_per_seq):
    ti = pl.program_id(0) % tiles_per_seq
    cb = _conv_tile(uc_ref, up_ref, un_ref, kern_ref, cbias_ref, clnw_ref, clnb_ref, ext_ref, sh_ref,
                    ti == 0, ti == tiles_per_seq - 1)
    x = x_ref[...]
    h = _rms(x, nmix_ref[...]).astype(bf16)
    gates = _sigmoid(jnp.dot(h, wg_ref[...], preferred_element_type=f32))
    wkv = yf_ref[...] + yb_ref[...]
    havg = havg_ref[...]
    mu = _mm_split_lhs(wkv, havg, 2)
    xc = wkv - mu
    var = _mm_split_lhs(xc * xc, havg, 2)
    y = (xc * lax.rsqrt(var + GN_EPS) * lnxw_ref[...] + lnxb_ref[...] + bonus_ref[...]) * g_ref[...]
    y_a = jnp.dot(y.astype(bf16), wor_ref[...], preferred_element_type=f32)
    y_b = jnp.dot(cb.astype(bf16), woc_ref[...], preferred_element_type=f32)
    m = gates[:, :D_MODEL] * y_a + gates[:, D_MODEL:] * y_b
    x1 = x + jnp.dot(m.astype(bf16), wout_ref[...], preferred_element_type=f32)
    x1_ref[...] = x1
    h2 = _rms(x1, nffn_ref[...])
    h2_ref[...] = h2
    logits = lax.dot_general(rwt_ref[...], h2.astype(bf16), NT, preferred_element_type=f32) + rb_ref[...]
    mx = jnp.max(logits, axis=0, keepdims=True)
    e = jnp.exp(logits - mx)
    aff_ref[...] = e / jnp.sum(e, axis=0, keepdims=True)


def _merge(t, x2, yf, yb, g, bonus, u2, kern, cbias, clnw, clnb, nmix, wg, lnxw, lnxb, havg, wor, woc, wout,
           nffn, rwt, rb):
    n = x2.shape[0]
    tm = _tile(t, 512)
    hb = tm // CONV_HALO
    row = lambda c: pl.BlockSpec((tm, c), lambda i: (i, 0))
    full = lambda s: pl.BlockSpec(s, lambda i: tuple(0 for _ in s))
    return pl.pallas_call(
        functools.partial(_merge_kernel, tiles_per_seq=t // tm),
        grid=(n // tm,),
        in_specs=[
            row(D_MODEL), row(D_RWKV), row(D_RWKV), row(D_RWKV), row(D_RWKV), row(2 * C_CONV),
            pl.BlockSpec((CONV_HALO, 2 * C_CONV), lambda i: (jnp.maximum(i * hb - 1, 0), 0)),
            pl.BlockSpec((CONV_HALO, 2 * C_CONV), lambda i: (jnp.minimum((i + 1) * hb, n // CONV_HALO - 1), 0)),
            full((CONV_W, C_CONV)), full((1, C_CONV)), full((1, C_CONV)), full((1, C_CONV)),
            full((1, D_MODEL)), full((D_MODEL, 2 * D_MODEL)), full((1, D_RWKV)), full((1, D_RWKV)),
            full((D_RWKV, D_RWKV)), full((D_RWKV, D_MODEL)), full((C_CONV, D_MODEL)),
            full((D_MODEL, D_MODEL)), full((1, D_MODEL)), full((N_EXPERTS, D_MODEL)), full((N_EXPERTS, 1)),
        ],
        out_specs=[row(D_MODEL), row(D_MODEL), pl.BlockSpec((N_EXPERTS, tm), lambda i: (0, i))],
        out_shape=[
            jax.ShapeDtypeStruct((n, D_MODEL), f32),
            jax.ShapeDtypeStruct((n, D_MODEL), f32),
            jax.ShapeDtypeStruct((N_EXPERTS, n), f32),
        ],
        scratch_shapes=[pltpu.VMEM((tm + 2 * CONV_HALO, C_CONV), f32),
                        pltpu.VMEM((SUBLANES_V7X, tm + 2 * CONV_HALO - SUBLANES_V7X, C_CONV), f32)],
        compiler_params=_cparams(("parallel",)),
        name="merge_router",
    )(x2, yf, yb, g, bonus, u2, u2, u2, kern, cbias, clnw, clnb, nmix, wg, lnxw, lnxb, havg, wor, woc, wout,
      nffn, rwt, rb)


def _thr_kernel(aff_ref, thr_ref, *, cap):
    def count_ge(mid):
        bits = pltpu.bitcast(aff_ref[...], i32)
        return jnp.sum(jnp.where(bits >= mid, 1.0, 0.0), axis=1, keepdims=True)

    def body(_, c):
        lo, hi = c
        mid = lo + ((hi - lo + 1) >> 1)
        ok = count_ge(mid) >= cap
        return jnp.where(ok, mid, lo), jnp.where(ok, hi, mid - 1)

    lo0 = jnp.zeros((N_EXPERTS, 1), i32)
    hi0 = jnp.full((N_EXPERTS, 1), 0x3F800000, i32)
    lo, _ = lax.fori_loop(0, 31, body, (lo0, hi0))
    thr_ref[...] = jnp.broadcast_to(lo, thr_ref.shape)


def _threshold(aff_t, cap):
    e, n = aff_t.shape
    return pl.pallas_call(
        functools.partial(_thr_kernel, cap=float(cap)),
        in_specs=[pl.BlockSpec((e, n), lambda: (0, 0))],
        out_specs=pl.BlockSpec((e, LANES_V7X), lambda: (0, 0)),
        out_shape=jax.ShapeDtypeStruct((e, LANES_V7X), i32),
        compiler_params=pltpu.CompilerParams(vmem_limit_bytes=VMEM_LIMIT_V7X),
        name="route_threshold",
    )(aff_t)


def _select(aff_t, cap):
    thr = _threshold(aff_t, cap)[:, :1]
    bits = lax.bitcast_convert_type(aff_t, i32)
    gt = bits > thr
    eq = bits == thr
    need = cap - jnp.sum(gt, axis=1, keepdims=True)
    sel = gt | (eq & (jnp.cumsum(eq, axis=1) <= need))
    cum = jnp.cumsum(sel, axis=1)
    slots = jnp.arange(1, cap + 1, dtype=cum.dtype)
    idx = jax.vmap(lambda c: jnp.searchsorted(c, slots, side="left"))(cum).astype(i32)
    gate = jnp.take_along_axis(aff_t, idx, axis=1)
    return idx, gate, sel, cum.astype(i32)


def _ffn_kernel(idx_ref, idxn_ref, gate_ref, h2_hbm, wg_ref, wu_ref, wd_ref, ye_ref, xbuf, sem):
    tc = xbuf.shape[1]
    s = pl.program_id(0)
    last = pl.num_programs(0) - 1
    slot = s % 2

    def x_copy(ref, i, sl):
        return pltpu.make_async_copy(h2_hbm.at[pl.ds(ref[0, 0, i], 1)], xbuf.at[sl, pl.ds(i, 1)], sem.at[sl])

    def wait_tile(sl):
        pltpu.make_async_copy(h2_hbm.at[pl.ds(0, tc)], xbuf.at[sl], sem.at[sl]).wait()

    @pl.when(s == 0)
    def _():
        def body(i, c):
            x_copy(idx_ref, i, 0).start()
            return c
        lax.fori_loop(0, tc, body, 0, unroll=8)

    wait_tile(slot)
    xe = xbuf[slot].astype(bf16)
    hg = jnp.dot(xe, wg_ref[0], preferred_element_type=f32)
    for i in range(tc // 2):
        x_copy(idxn_ref, i, 1 - slot).start(priority=i % 2)
    hu = jnp.dot(xe, wu_ref[0], preferred_element_type=f32)
    hid = (hg * _sigmoid(hg) * hu).astype(bf16)
    for i in range(tc // 2, tc):
        x_copy(idxn_ref, i, 1 - slot).start(priority=i % 2)
    ye_ref[...] = (jnp.dot(hid, wd_ref[0], preferred_element_type=f32) * gate_ref[0]).astype(bf16)

    @pl.when(s == last)
    def _():
        wait_tile(1 - slot)


def _ffn(idx, gate, h2, wg, wu, wd):
    e, cap = idx.shape
    tc = _tile(cap, 256)
    nt = cap // tc
    steps = e * nt
    wspec = pl.BlockSpec((1, D_MODEL, D_FF), lambda si: (si // nt, 0, 0))
    idx3 = idx.reshape(steps, 1, tc)
    return pl.pallas_call(
        _ffn_kernel,
        grid=(steps,),
        in_specs=[
            pl.BlockSpec((1, 1, tc), lambda si: (si, 0, 0), memory_space=pltpu.SMEM),
            pl.BlockSpec((1, 1, tc), lambda si: (jnp.minimum(si + 1, steps - 1), 0, 0), memory_space=pltpu.SMEM),
            pl.BlockSpec((1, tc, 1), lambda si: (si, 0, 0)),
            pl.BlockSpec(memory_space=pl.ANY),
            wspec, wspec,
            pl.BlockSpec((1, D_FF, D_MODEL), lambda si: (si // nt, 0, 0)),
        ],
        out_specs=pl.BlockSpec((tc, D_MODEL), lambda si: (si, 0)),
        out_shape=jax.ShapeDtypeStruct((e * cap, D_MODEL), bf16),
        scratch_shapes=[pltpu.VMEM((2, tc, D_MODEL), f32), pltpu.SemaphoreType.DMA((2,))],
        compiler_params=_cparams(("arbitrary",)),
        name="expert_ffn",
    )(idx3, idx3, gate.reshape(steps, tc, 1), h2, wg, wu, wd)


COMB_CH = 32
COMB_KB = 256
COMB_ALIGN = 2 * SUBLANES_V7X
META_START, META_NCH, META_OFF, META_SLOT, META_TOT = 0, 16, 32, 48, 64
META_W = 128
NOT_ROUTED = -(1 << 30)


def _combine_kernel(meta_ref, metan_ref, x1_ref, pos_ref, ye_hbm, g_ref, o_ref, slab, oht, acc, sem):
    tn = x1_ref.shape[0]
    i = pl.program_id(0)
    last = pl.num_programs(0) - 1
    buf = i % 2

    def pieces(mref, b, fn):
        for e in range(N_EXPERTS):
            def body(c, carry, e=e):
                src = ye_hbm.at[pl.ds(pl.multiple_of(mref[0, 0, META_START + e] + c * COMB_CH, COMB_ALIGN),
                                      COMB_CH)]
                dst = slab.at[b, pl.ds(pl.multiple_of(mref[0, 0, META_OFF + e] + c * COMB_CH, COMB_CH), COMB_CH)]
                fn(pltpu.make_async_copy(src, dst, sem.at[b]))
                return carry
            lax.fori_loop(0, mref[0, 0, META_NCH + e], body, 0)

    @pl.when(i == 0)
    def _():
        slab[...] = jnp.zeros_like(slab)
        pieces(meta_ref, 0, lambda cp: cp.start())

    @pl.when(i < last)
    def _():
        pieces(metan_ref, 1 - buf, lambda cp: cp.start())

    rowi = lax.broadcasted_iota(i32, (COMB_CH, 1), 0)
    for e in range(N_EXPERTS):
        pos_e = pos_ref[e:e + 1, :]

        def fill(c, carry, e=e, pos_e=pos_e):
            slot = meta_ref[0, 0, META_SLOT + e] + c * COMB_CH + rowi
            row0 = pl.multiple_of(meta_ref[0, 0, META_OFF + e] + c * COMB_CH, COMB_CH)
            oht[pl.ds(row0, COMB_CH), :] = jnp.where(pos_e == slot, 1.0, 0.0).astype(bf16)
            return carry
        lax.fori_loop(0, meta_ref[0, 0, META_NCH + e], fill, 0)

    ktot = meta_ref[0, 0, META_TOT]
    nkb = (ktot + COMB_KB - 1) // COMB_KB

    def clear(c, carry):
        row0 = pl.multiple_of(ktot + c * COMB_CH, COMB_CH)
        oht[pl.ds(row0, COMB_CH), :] = jnp.zeros((COMB_CH, tn), bf16)
        return carry
    lax.fori_loop(0, (nkb * COMB_KB - ktot) // COMB_CH, clear, 0)

    pieces(meta_ref, buf, lambda cp: cp.wait())
    acc[...] = x1_ref[...]

    def mm(j, carry):
        rows = pl.ds(pl.multiple_of(j * COMB_KB, COMB_KB), COMB_KB)
        acc[...] += lax.dot_general(oht[rows, :], slab[buf, rows, :], TN, preferred_element_type=f32)
        return carry
    lax.fori_loop(0, nkb, mm, 0)
    o_ref[...] = _rms(acc[...], g_ref[...])


def _combine(meta, x1, pos, ye, g):
    n = x1.shape[0]
    tn = _tile(n, 256)
    nt = n // tn
    kmax = N_EXPERTS * (tn + 2 * COMB_CH)
    assert kmax % COMB_KB == 0
    return pl.pallas_call(
        _combine_kernel,
        grid=(nt,),
        in_specs=[
            pl.BlockSpec((1, 1, META_W), lambda i: (i, 0, 0), memory_space=pltpu.SMEM),
            pl.BlockSpec((1, 1, META_W), lambda i: (jnp.minimum(i + 1, nt - 1), 0, 0), memory_space=pltpu.SMEM),
            pl.BlockSpec((tn, D_MODEL), lambda i: (i, 0)),
            pl.BlockSpec((N_EXPERTS, tn), lambda i: (0, i)),
            pl.BlockSpec(memory_space=pl.ANY),
            pl.BlockSpec((1, D_MODEL), lambda i: (0, 0)),
        ],
        out_specs=pl.BlockSpec((tn, D_MODEL), lambda i: (i, 0)),
        out_shape=jax.ShapeDtypeStruct((n, D_MODEL), f32),
        scratch_shapes=[pltpu.VMEM((2, kmax, D_MODEL), bf16), pltpu.VMEM((kmax, tn), bf16),
                        pltpu.VMEM((tn, D_MODEL), f32), pltpu.SemaphoreType.DMA((2,))],
        compiler_params=_cparams(("arbitrary",)),
        name="combine_norm",
    )(meta, meta, x1, pos, ye, g)


def _combine_meta(sel, cum, cap, tn):
    e, n = cum.shape
    nt = n // tn
    end = cum[:, tn - 1::tn]
    before = jnp.concatenate([jnp.zeros((e, 1), i32), end[:, :-1]], axis=1)
    base = jnp.arange(e, dtype=i32)[:, None] * cap
    start = (base + before) // COMB_ALIGN * COMB_ALIGN
    nch = jnp.where(end > before, (base + end - start + COMB_CH - 1) // COMB_CH, 0)
    start = jnp.minimum(start, e * cap - nch * COMB_CH)
    rows = nch * COMB_CH
    off = jnp.cumsum(rows, axis=0) - rows
    tot = jnp.sum(rows, axis=0, keepdims=True)
    pad = jnp.zeros((META_W - META_TOT - 1, nt), i32)
    meta = jnp.concatenate([start, nch, off, start - base, tot, pad], axis=0)
    pos = jnp.where(sel, cum - 1, NOT_ROUTED).astype(i32)
    return meta.T.reshape(nt, 1, META_W), pos


def _prepare_params(norm_mix, w_in, shift_mu_prev, shift_mu_next, decay_w0, decay_up, iclr_a0, iclr_up,
                    gate_up, k_k, k_a, r_k, lnx_w, lnx_b, w_o_rwkv, dw_kernel, dw_bias, conv_ln_w, conv_ln_b,
                    w_o_conv, w_out, norm_ffn, router_w, router_b, exp_w_gate, exp_w_up, exp_w_down, norm_final):
    l = 0
    rwkv_cols = RKV_COLS + LORA_COLS
    w = w_in[l]
    pad = jnp.zeros((D_MODEL, LORA_PAD - LORA_COLS), f32)
    wa = jnp.concatenate([w[:, :rwkv_cols], pad], axis=1).astype(bf16)
    wu = w[:, rwkv_cols:rwkv_cols + 2 * C_CONV].astype(bf16)
    wgate = w[:, rwkv_cols + 2 * C_CONV:].astype(bf16)
    padv = jnp.zeros((LORA_PAD - LORA_COLS,), f32)
    mup = jnp.concatenate([shift_mu_prev[l], padv])[None, :]
    mun = jnp.concatenate([shift_mu_next[l], padv])[None, :]
    wup = jnp.zeros((LORA_PAD, UP_COLS), f32)
    wup = wup.at[0:W_LORA, 0:D_RWKV].set(decay_up[l, 0])
    wup = wup.at[W_LORA:2 * W_LORA, D_RWKV:2 * D_RWKV].set(decay_up[l, 1])
    o = 2 * W_LORA
    wup = wup.at[o:o + A_LORA, 2 * D_RWKV:3 * D_RWKV].set(iclr_up[l, 0])
    wup = wup.at[o + A_LORA:o + 2 * A_LORA, 3 * D_RWKV:4 * D_RWKV].set(iclr_up[l, 1])
    o = 2 * W_LORA + 2 * A_LORA
    wup = wup.at[o:o + G_LORA, 4 * D_RWKV:5 * D_RWKV].set(gate_up[l])
    head = jnp.arange(D_RWKV) // HEAD
    same = (head[:, None] == head[None, :])
    return dict(
        norm_mix=norm_mix[l][None, :], wa=wa, wu=wu, wgate=wgate, mup=mup, mun=mun, wup=wup.astype(bf16),
        w0=decay_w0[l], a0=iclr_a0[l], kkw=k_k[l][None, :], ka=k_a[l][None, :], rk=r_k[l].reshape(1, D_RWKV),
        hsum=same.astype(bf16), havg=(same.astype(f32) / HEAD).astype(bf16),
        lnxw=lnx_w[l][None, :], lnxb=lnx_b[l][None, :], wor=w_o_rwkv[l].astype(bf16),
        kern=dw_kernel[l], cbias=dw_bias[l][None, :], clnw=conv_ln_w[l][None, :], clnb=conv_ln_b[l][None, :],
        woc=w_o_conv[l].astype(bf16), wout=w_out[l].astype(bf16), nffn=norm_ffn[l][None, :],
        rwt=router_w[l].T.astype(bf16), rb=router_b[l][:, None],
        ewg=exp_w_gate[l].astype(bf16), ewu=exp_w_up[l].astype(bf16), ewd=exp_w_down[l].astype(bf16),
        nfinal=norm_final[None, :],
    )


def _trunk(x, P):
    b, t, d = x.shape
    n = b * t
    x2 = x.reshape(n, d)
    p, u = _inproj(x2, P["norm_mix"], P["wa"], P["wu"])
    r, v, kk, ew0, ew1, k0, k1, b0, b1, g, bonus = _prep(
        p.reshape(b, t, PA_COLS), P["mup"], P["mun"], P["wup"], P["w0"], P["a0"], P["kkw"], P["ka"], P["rk"],
        P["hsum"])
    yf, yb = _scan(r, v, kk, ew0, ew1, k0, k1, b0, b1)
    flat = lambda a: a.reshape(n, a.shape[-1])
    x1, h2, aff_t = _merge(t, x2, flat(yf), flat(yb), flat(g), flat(bonus), u, P["kern"], P["cbias"],
                           P["clnw"], P["clnb"], P["norm_mix"], P["wgate"], P["lnxw"], P["lnxb"], P["havg"],
                           P["wor"], P["woc"], P["wout"], P["nffn"], P["rwt"], P["rb"])
    cap = max(1, CAPACITY_FACTOR * n // N_EXPERTS)
    idx, gate, sel, cum = _select(aff_t, cap)
    ye = _ffn(idx, gate, h2, P["ewg"], P["ewu"], P["ewd"])
    meta, pos = _combine_meta(sel, cum, cap, _tile(n, 256))
    return _combine(meta, x1, pos, ye, P["nfinal"]).reshape(b, t, d)


def kernel(x_prompt, x_sample, norm_mix, w_in, shift_mu_prev, shift_mu_next, decay_w0, decay_up, iclr_a0, iclr_up, gate_up, k_k, k_a, r_k, lnx_w, lnx_b, w_o_rwkv, dw_kernel, dw_bias, conv_ln_w, conv_ln_b, w_o_conv, w_out, norm_ffn, router_w, router_b, exp_w_gate, exp_w_up, exp_w_down, norm_final):
    P = _prepare_params(norm_mix, w_in, shift_mu_prev, shift_mu_next, decay_w0, decay_up, iclr_a0, iclr_up,
                        gate_up, k_k, k_a, r_k, lnx_w, lnx_b, w_o_rwkv, dw_kernel, dw_bias, conv_ln_w,
                        conv_ln_b, w_o_conv, w_out, norm_ffn, router_w, router_b, exp_w_gate, exp_w_up,
                        exp_w_down, norm_final)
    return (_trunk(x_prompt, P), _trunk(x_sample, P))
```

```python
import functools
import math

import jax
import jax.numpy as jnp
from jax import lax
from jax.experimental import pallas as pl
from jax.experimental.pallas import tpu as pltpu

f32 = jnp.float32
bf16 = jnp.bfloat16
i32 = jnp.int32

D_MODEL = 1024
D_RWKV = 512
HEAD = 64
W_LORA = 32
A_LORA = 32
G_LORA = 96
C_CONV = 512
CONV_W = 31
N_EXPERTS = 16
D_FF = 1024
CAPACITY_FACTOR = 2
GN_EPS = 64e-5
LN_EPS = 1e-5
RMS_EPS = 1e-6
LORA_COLS = 2 * W_LORA + 2 * A_LORA + G_LORA
LORA_PAD = 256
RKV_COLS = 3 * D_RWKV
PA_COLS = RKV_COLS + LORA_PAD
UP_COLS = 5 * D_RWKV

LANES_V7X = 128
SUBLANES_V7X = 8
MXU_DIM_V7X = 256
VMEM_LIMIT_V7X = 56 * 1024 * 1024

CHUNK = 64
SCAN_UNROLL = 2
SCAN_BLOCK = 512
GROUP = MXU_DIM_V7X // HEAD
GROUP_LANES = GROUP * HEAD
N_GROUPS = D_RWKV // GROUP_LANES

NN = (((1,), (0,)), ((), ()))
NT = (((1,), (1,)), ((), ()))
TN = (((0,), (0,)), ((), ()))


def _mm(a, b, dims=NN):
    return lax.dot_general(a.astype(bf16), b.astype(bf16), dims, preferred_element_type=f32)


def _mm_split(a, x, n):
    out = None
    rem = x
    for _ in range(n):
        hi = rem.astype(bf16)
        t = lax.dot_general(a, hi, NN, preferred_element_type=f32)
        out = t if out is None else out + t
        rem = rem - hi.astype(f32)
    return out


def _mm_split_lhs(x, a, n):
    out = None
    rem = x
    for _ in range(n):
        hi = rem.astype(bf16)
        t = lax.dot_general(hi, a, NN, preferred_element_type=f32)
        out = t if out is None else out + t
        rem = rem - hi.astype(f32)
    return out


def _sigmoid(x):
    return 1.0 / (1.0 + jnp.exp(-x))


def _rms(x, g):
    return x * lax.rsqrt(jnp.mean(x * x, axis=-1, keepdims=True) + RMS_EPS) * g


def _cparams(sem):
    return pltpu.CompilerParams(dimension_semantics=sem, vmem_limit_bytes=VMEM_LIMIT_V7X)


def _tile(n, pref):
    t = min(n, pref)
    assert n % t == 0, (n, pref)
    return t


def _inproj_kernel(x_ref, g_ref, wa_ref, wu_ref, p_ref, u_ref):
    h = _rms(x_ref[...], g_ref[...]).astype(bf16)
    p_ref[...] = jnp.dot(h, wa_ref[...], preferred_element_type=f32).astype(bf16)
    u_ref[...] = jnp.dot(h, wu_ref[...], preferred_element_type=f32).astype(bf16)


def _inproj(x2, g, wa, wu):
    n = x2.shape[0]
    tm = _tile(n, 512)
    return pl.pallas_call(
        _inproj_kernel,
        grid=(n // tm,),
        in_specs=[
            pl.BlockSpec((tm, D_MODEL), lambda i: (i, 0)),
            pl.BlockSpec((1, D_MODEL), lambda i: (0, 0)),
            pl.BlockSpec((D_MODEL, PA_COLS), lambda i: (0, 0)),
            pl.BlockSpec((D_MODEL, 2 * C_CONV), lambda i: (0, 0)),
        ],
        out_specs=[
            pl.BlockSpec((tm, PA_COLS), lambda i: (i, 0)),
            pl.BlockSpec((tm, 2 * C_CONV), lambda i: (i, 0)),
        ],
        out_shape=[
            jax.ShapeDtypeStruct((n, PA_COLS), bf16),
            jax.ShapeDtypeStruct((n, 2 * C_CONV), bf16),
        ],
        compiler_params=_cparams(("parallel",)),
        name="inproj",
    )(x2, g, wa, wu)


PREP_HALO = 16


def _prep_kernel(pc_ref, pp_ref, pn_ref, mup_ref, mun_ref, wup_ref, w0_ref, a0_ref, kkw_ref, ka_ref,
                 rk_ref, hsum_ref, r_o, v_o, kk_o, ew0_o, ew1_o, k0_o, k1_o, b0_o, b1_o, g_o, bonus_o):
    tt = pc_ref.shape[1]
    i = pl.program_id(1)
    nt = pl.num_programs(1)
    row = lax.broadcasted_iota(i32, (tt, 1), 0)
    first = i == 0
    last = i == nt - 1

    def shifted(c0, c1):
        p = pc_ref[0, :, c0:c1].astype(f32)
        prev_row = jnp.where(first, 0.0, pp_ref[0, PREP_HALO - 1:PREP_HALO, c0:c1].astype(f32))
        next_row = jnp.where(last, 0.0, pn_ref[0, 0:1, c0:c1].astype(f32))
        zp = jnp.where(row == 0, prev_row, pltpu.roll(p, 1, axis=0))
        zn = jnp.where(row == tt - 1, next_row, pltpu.roll(p, tt - 1, axis=0))
        return p + mup_ref[:, c0:c1] * (zp - p) + mun_ref[:, c0:c1] * (zn - p)

    r = shifted(0, D_RWKV)
    k = shifted(D_RWKV, 2 * D_RWKV)
    v = shifted(2 * D_RWKV, 3 * D_RWKV)
    zl = shifted(RKV_COLS, PA_COLS)

    lane = lax.broadcasted_iota(i32, (1, LORA_PAD), 1)
    act = jnp.where(lane < 2 * W_LORA, jnp.tanh(zl),
                    jnp.where(lane < 2 * W_LORA + 2 * A_LORA, zl, _sigmoid(zl)))
    pre = jnp.dot(act.astype(bf16), wup_ref[...], preferred_element_type=f32)

    hsum = hsum_ref[...]
    kkr = k * kkw_ref[...]
    ss = _mm_split_lhs(kkr * kkr, hsum, 2)
    kk = kkr / jnp.maximum(jnp.sqrt(ss), 1e-12)
    ka = ka_ref[...]
    ksum = None
    outs = ((ew0_o, k0_o, b0_o), (ew1_o, k1_o, b1_o))
    for d in range(2):
        x = pre[:, d * D_RWKV:(d + 1) * D_RWKV] + w0_ref[d:d + 1, :]
        ew = math.exp(-0.5) * _sigmoid(x)
        alpha = _sigmoid(pre[:, (2 + d) * D_RWKV:(3 + d) * D_RWKV] + a0_ref[d:d + 1, :])
        kd = k * (1.0 + (alpha - 1.0) * ka)
        ew_o, k_o, b_o = outs[d]
        ew_o[0] = ew
        k_o[0] = kd.astype(bf16)
        b_o[0] = (kk * alpha).astype(bf16)
        ksum = kd if ksum is None else ksum + kd
    bonus = _mm_split_lhs(r * ksum * rk_ref[...], hsum, 2) * v
    r_o[0] = r.astype(bf16)
    v_o[0] = v.astype(bf16)
    kk_o[0] = kk.astype(bf16)
    g_o[0] = pre[:, 4 * D_RWKV:5 * D_RWKV].astype(bf16)
    bonus_o[0] = bonus.astype(bf16)


def _prep(p3, mup, mun, wup, w0, a0, kkw, ka, rk, hsum):
    b, t, _ = p3.shape
    tt = _tile(t, 256)
    nt = t // tt
    hb = tt // PREP_HALO
    full = lambda s: pl.BlockSpec(s, lambda bi, ti: tuple(0 for _ in s))
    o_spec = pl.BlockSpec((1, tt, D_RWKV), lambda bi, ti: (bi, ti, 0))
    o_lo = jax.ShapeDtypeStruct((b, t, D_RWKV), bf16)
    o_hi = jax.ShapeDtypeStruct((b, t, D_RWKV), f32)
    return pl.pallas_call(
        _prep_kernel,
        grid=(b, nt),
        in_specs=[
            pl.BlockSpec((1, tt, PA_COLS), lambda bi, ti: (bi, ti, 0)),
            pl.BlockSpec((1, PREP_HALO, PA_COLS), lambda bi, ti: (bi, jnp.maximum(ti * hb - 1, 0), 0)),
            pl.BlockSpec((1, PREP_HALO, PA_COLS),
                         lambda bi, ti: (bi, jnp.minimum((ti + 1) * hb, t // PREP_HALO - 1), 0)),
            full((1, PA_COLS)), full((1, PA_COLS)), full((LORA_PAD, UP_COLS)),
            full((2, D_RWKV)), full((2, D_RWKV)), full((1, D_RWKV)), full((1, D_RWKV)), full((1, D_RWKV)),
            full((D_RWKV, D_RWKV)),
        ],
        out_specs=[o_spec] * 11,
        out_shape=[o_lo, o_lo, o_lo, o_hi, o_hi] + [o_lo] * 6,
        compiler_params=_cparams(("parallel", "parallel")),
        name="rwkv_prep",
    )(p3, p3, p3, mup, mun, wup, w0, a0, kkw, ka, rk, hsum)


def _scan_consts(reverse):
    L = CHUNK
    t = lax.broadcasted_iota(i32, (L, GROUP * L), 0)
    s = lax.broadcasted_iota(i32, (L, GROUP * L), 1) & (L - 1)
    strict = (s > t) if reverse else (s < t)
    eye = s == t
    incl = strict | eye
    tt = lax.broadcasted_iota(i32, (L, L), 0)
    ss = lax.broadcasted_iota(i32, (L, L), 1)
    tri = jnp.where((ss >= tt) if reverse else (ss <= tt), 1.0, 0.0).astype(bf16)
    rr = lax.broadcasted_iota(i32, (GROUP_LANES, GROUP_LANES), 0) // HEAD
    cc = lax.broadcasted_iota(i32, (GROUP_LANES, GROUP_LANES), 1) // HEAD
    same = rr == cc
    return strict, incl, eye, tri, (same, jnp.where(same, 1.0, 0.0).astype(bf16))


def _bd(x, bdmask):
    xb = x.astype(bf16)
    return jnp.concatenate([xb] * GROUP, axis=0) * bdmask[1]


def _wkv_chunks(chains):
    L = CHUNK
    n = len(chains)
    rng = range(n)
    r, v, kk, ew, k, b, consts, rev = (list(c) for c in zip(*chains))
    strict = [c[0] for c in consts]
    incl = [c[1] for c in consts]
    eye = [c[2] for c in consts]
    tri = [c[3] for c in consts]
    bdm = consts[0][4]
    ginc = [-_mm_split(tri[i], ew[i], 3) for i in rng]
    gref = [0.5 * (ginc[i][0:1] if rev[i] else ginc[i][L - 1:L]) for i in rng]
    egref = [jnp.exp(gref[i]) for i in rng]
    e1 = [jnp.exp(ginc[i] - gref[i]) for i in rng]
    e2 = [jnp.exp(gref[i] - ginc[i]) for i in rng]
    ea = [jnp.exp(ginc[i] + ew[i] - gref[i]) for i in rng]
    At = [-kk[i] * ea[i] for i in rng]
    Rt = [r[i] * e1[i] for i in rng]
    Kt = [k[i] * e2[i] for i in rng]
    Bt = [b[i] * e2[i] for i in rng]
    Be = [Bt[i] * egref[i] for i in rng]
    AR = [jnp.concatenate([At[i], Rt[i]], axis=0) for i in rng]
    pb = [_mm(AR[i], _bd(Bt[i], bdm), NT) for i in rng]
    pk = [_mm(AR[i], _bd(Kt[i], bdm), NT) for i in rng]
    Mab = [jnp.where(strict[i], pb[i][:L], 0.0) for i in rng]
    Arb = [jnp.where(incl[i], pb[i][L:], 0.0) for i in rng]
    Mak = [jnp.where(strict[i], pk[i][:L], 0.0) for i in rng]
    Ark = [jnp.where(incl[i], pk[i][L:], 0.0) for i in rng]
    vbd = [_bd(v[i], bdm) for i in rng]
    MakV = [_mm(Mak[i], vbd[i]) for i in rng]
    X = [jnp.where(eye[i], 1.0, Mab[i]) for i in rng]
    P = [_mm(Mab[i], _bd(Mab[i], bdm)) for i in rng]
    for step in range(4):
        XP = [_mm(jnp.concatenate([X[i], P[i]], axis=0), _bd(P[i], bdm)) for i in rng]
        X = [X[i] + XP[i][:L] for i in rng]
        P = [XP[i][L:] for i in rng]
    X = [X[i] + _mm(X[i], _bd(P[i], bdm)) for i in rng]
    W1 = [_mm(X[i], _bd(At[i] * egref[i], bdm)) for i in rng]
    W2 = [_mm(X[i], _bd(MakV[i], bdm)) for i in rng]
    Q = [Rt[i] * egref[i] + _mm(Arb[i], _bd(W1[i], bdm)) for i in rng]
    Y0 = [_mm(Arb[i], _bd(W2[i], bdm)) + _mm(Ark[i], vbd[i]) for i in rng]
    Pm = [jnp.where(bdm[0], _mm(W1[i], Be[i], TN), 0.0) for i in rng]
    Qm = [jnp.where(bdm[0], _mm(jnp.concatenate([W2[i], v[i]], axis=0),
                                jnp.concatenate([Be[i], Kt[i] * egref[i]], axis=0), TN), 0.0) for i in rng]
    D = [egref[i] * egref[i] for i in rng]
    return Q, Y0, Pm, Qm, D


def _scan_kernel(rf, vf, kkf, ewf, kf, bf, rb, vb, kkb, ewb, kb, bb, yf_ref, yb_ref, s_ref):
    tb = rf.shape[1]
    nch = tb // CHUNK

    @pl.when(pl.program_id(1) == 0)
    def _():
        s_ref[...] = jnp.zeros_like(s_ref)

    consts_f = _scan_consts(False)
    consts_b = _scan_consts(True)

    def body(j, carry):
        items, dests = [], []
        for sub in range(SCAN_UNROLL):
            for reverse, refs, y_ref, consts in ((False, (rf, vf, kkf, ewf, kf, bf), yf_ref, consts_f),
                                                 (True, (rb, vb, kkb, ewb, kb, bb), yb_ref, consts_b)):
                step = j * SCAN_UNROLL + sub
                cj = (nch - 1 - step) if reverse else step
                rows = pl.ds(pl.multiple_of(cj * CHUNK, CHUNK), CHUNK)
                for g in range(N_GROUPS):
                    lanes = slice(g * GROUP_LANES, (g + 1) * GROUP_LANES)
                    items.append(tuple(ref[0, rows, lanes].astype(f32) for ref in refs) + (consts, reverse))
                    dests.append((y_ref, rows, lanes, (2 if reverse else 0) + g))
        Q, Y0, Pm, Qm, D = _wkv_chunks(items)
        S = [s_ref[si] for si in range(2 * N_GROUPS)]
        for i, (y_ref, rows, lanes, si) in enumerate(dests):
            y_ref[0, rows, lanes] = _mm(Q[i], S[si], NT) + Y0[i]
            S[si] = S[si] * D[i] + _mm(S[si], Pm[i]) + Qm[i]
        for si in range(2 * N_GROUPS):
            s_ref[si] = S[si]
        return carry

    lax.fori_loop(0, nch // SCAN_UNROLL, body, 0)


def _scan(r, v, kk, ew0, ew1, k0, k1, b0, b1):
    b, t, _ = r.shape
    tb = _tile(t, SCAN_BLOCK)
    assert tb % (CHUNK * SCAN_UNROLL) == 0
    nb = t // tb
    fwd = pl.BlockSpec((1, tb, D_RWKV), lambda bi, ti: (bi, ti, 0))
    bwd = pl.BlockSpec((1, tb, D_RWKV), lambda bi, ti: (bi, nb - 1 - ti, 0))
    o_shape = jax.ShapeDtypeStruct((b, t, D_RWKV), f32)
    return pl.pallas_call(
        _scan_kernel,
        grid=(b, nb),
        in_specs=[fwd] * 6 + [bwd] * 6,
        out_specs=[fwd, bwd],
        out_shape=[o_shape, o_shape],
        scratch_shapes=[pltpu.VMEM((2 * N_GROUPS, GROUP_LANES, GROUP_LANES), f32)],
        compiler_params=_cparams(("parallel", "arbitrary")),
        name="wkv_scan",
    )(r, v, kk, ew0, k0, b0, r, v, kk, ew1, k1, b1)


CONV_HALO = 16


def _conv_tile(uc_ref, up_ref, un_ref, kern_ref, bias_ref, lnw_ref, lnb_ref, ext_ref, sh_ref, first, last):
    tt = uc_ref.shape[0]

    def glu(u):
        u = u.astype(f32)
        return u[:, :C_CONV] * _sigmoid(u[:, C_CONV:])

    ext_ref[0:CONV_HALO, :] = jnp.where(first, 0.0, glu(up_ref[...]))
    ext_ref[CONV_HALO:CONV_HALO + tt, :] = glu(uc_ref[...])
    ext_ref[CONV_HALO + tt:, :] = jnp.where(last, 0.0, glu(un_ref[...]))
    span = sh_ref.shape[1]
    for r in range(1, SUBLANES_V7X):
        sh_ref[r] = ext_ref[r:r + span, :]
    acc = jnp.zeros((tt, C_CONV), f32) + bias_ref[...]
    half = CONV_W // 2
    for j in range(CONV_W):
        off = CONV_HALO - half + j
        r, base = off % SUBLANES_V7X, off - off % SUBLANES_V7X
        tap = ext_ref[base:base + tt, :] if r == 0 else sh_ref[r, base:base + tt, :]
        acc = acc + kern_ref[j:j + 1, :] * tap
    mu = jnp.mean(acc, axis=-1, keepdims=True)
    xc = acc - mu
    var = jnp.mean(xc * xc, axis=-1, keepdims=True)
    y = xc * lax.rsqrt(var + LN_EPS) * lnw_ref[...] + lnb_ref[...]
    return y * _sigmoid(y)


def _merge_kernel(x_ref, yf_ref, yb_ref, g_ref, bonus_ref, uc_ref, up_ref, un_ref, kern_ref, cbias_ref,
                  clnw_ref, clnb_ref, nmix_ref, wg_ref, lnxw_ref, lnxb_ref,
                  havg_ref, wor_ref, woc_ref, wout_ref, nffn_ref, rwt_ref, rb_ref,
                  x1_ref, h2_ref, aff_ref, ext_ref, sh_ref, *, tiles_per_seq):
    ti = pl.program_id(0) % tiles_per_seq
    cb = _conv_tile(uc_ref, up_ref, un_ref, kern_ref, cbias_ref, clnw_ref, clnb_ref, ext_ref, sh_ref,
                    ti == 0, ti == tiles_per_seq - 1)
    x = x_ref[...]
    h = _rms(x, nmix_ref[...]).astype(bf16)
    gates = _sigmoid(jnp.dot(h, wg_ref[...], preferred_element_type=f32))
    wkv = yf_ref[...] + yb_ref[...]
    havg = havg_ref[...]
    mu = _mm_split_lhs(wkv, havg, 2)
    xc = wkv - mu
    var = _mm_split_lhs(xc * xc, havg, 2)
    y = (xc * lax.rsqrt(var + GN_EPS) * lnxw_ref[...] + lnxb_ref[...] + bonus_ref[...]) * g_ref[...]
    y_a = jnp.dot(y.astype(bf16), wor_ref[...], preferred_element_type=f32)
    y_b = jnp.dot(cb.astype(bf16), woc_ref[...], preferred_element_type=f32)
    m = gates[:, :D_MODEL] * y_a + gates[:, D_MODEL:] * y_b
    x1 = x + jnp.dot(m.astype(bf16), wout_ref[...], preferred_element_type=f32)
    x1_ref[...] = x1
    h2 = _rms(x1, nffn_ref[...]).astype(bf16)
    hbits = pltpu.bitcast(h2.astype(f32), jnp.uint32)
    half = D_MODEL // 2
    h2_ref[...] = (hbits[:, :half] >> 16) | (hbits[:, half:] & jnp.uint32(0xFFFF0000))
    logits = lax.dot_general(rwt_ref[...], h2, NT, preferred_element_type=f32) + rb_ref[...]
    mx = jnp.max(logits, axis=0, keepdims=True)
    e = jnp.exp(logits - mx)
    aff_ref[...] = e / jnp.sum(e, axis=0, keepdims=True)


def _merge(t, x2, yf, yb, g, bonus, u2, kern, cbias, clnw, clnb, nmix, wg, lnxw, lnxb, havg, wor, woc, wout,
           nffn, rwt, rb):
    n = x2.shape[0]
    tm = _tile(t, 512)
    hb = tm // CONV_HALO
    row = lambda c: pl.BlockSpec((tm, c), lambda i: (i, 0))
    full = lambda s: pl.BlockSpec(s, lambda i: tuple(0 for _ in s))
    return pl.pallas_call(
        functools.partial(_merge_kernel, tiles_per_seq=t // tm),
        grid=(n // tm,),
        in_specs=[
            row(D_MODEL), row(D_RWKV), row(D_RWKV), row(D_RWKV), row(D_RWKV), row(2 * C_CONV),
            pl.BlockSpec((CONV_HALO, 2 * C_CONV), lambda i: (jnp.maximum(i * hb - 1, 0), 0)),
            pl.BlockSpec((CONV_HALO, 2 * C_CONV), lambda i: (jnp.minimum((i + 1) * hb, n // CONV_HALO - 1), 0)),
            full((CONV_W, C_CONV)), full((1, C_CONV)), full((1, C_CONV)), full((1, C_CONV)),
            full((1, D_MODEL)), full((D_MODEL, 2 * D_MODEL)), full((1, D_RWKV)), full((1, D_RWKV)),
            full((D_RWKV, D_RWKV)), full((D_RWKV, D_MODEL)), full((C_CONV, D_MODEL)),
            full((D_MODEL, D_MODEL)), full((1, D_MODEL)), full((N_EXPERTS, D_MODEL)), full((N_EXPERTS, 1)),
        ],
        out_specs=[row(D_MODEL), row(D_MODEL // 2), pl.BlockSpec((N_EXPERTS, tm), lambda i: (0, i))],
        out_shape=[
            jax.ShapeDtypeStruct((n, D_MODEL), f32),
            jax.ShapeDtypeStruct((n, D_MODEL // 2), jnp.uint32),
            jax.ShapeDtypeStruct((N_EXPERTS, n), f32),
        ],
        scratch_shapes=[pltpu.VMEM((tm + 2 * CONV_HALO, C_CONV), f32),
                        pltpu.VMEM((SUBLANES_V7X, tm + 2 * CONV_HALO - SUBLANES_V7X, C_CONV), f32)],
        compiler_params=_cparams(("parallel",)),
        name="merge_router",
    )(x2, yf, yb, g, bonus, u2, u2, u2, kern, cbias, clnw, clnb, nmix, wg, lnxw, lnxb, havg, wor, woc, wout,
      nffn, rwt, rb)


def _thr_kernel(aff_ref, thr_ref, need_ref, *, cap):
    def count_ge(mid):
        bits = pltpu.bitcast(aff_ref[...], i32)
        return jnp.sum(jnp.where(bits >= mid, 1.0, 0.0), axis=1, keepdims=True)

    def body(_, c):
        lo, hi = c
        mid = lo + ((hi - lo + 1) >> 1)
        ok = count_ge(mid) >= cap
        return jnp.where(ok, mid, lo), jnp.where(ok, hi, mid - 1)

    lo0 = jnp.zeros((N_EXPERTS, 1), i32)
    hi0 = jnp.full((N_EXPERTS, 1), 0x3F800000, i32)
    lo, _ = lax.fori_loop(0, 31, body, (lo0, hi0))
    thr_ref[...] = jnp.broadcast_to(lo, thr_ref.shape)
    bits = pltpu.bitcast(aff_ref[...], i32)
    need = cap - jnp.sum(jnp.where(bits > lo, 1.0, 0.0), axis=1, keepdims=True)
    need_ref[...] = jnp.broadcast_to(need, need_ref.shape)


def _threshold(aff_t, cap):
    e, n = aff_t.shape
    o_spec = pl.BlockSpec((e, LANES_V7X), lambda: (0, 0))
    return pl.pallas_call(
        functools.partial(_thr_kernel, cap=float(cap)),
        in_specs=[pl.BlockSpec((e, n), lambda: (0, 0))],
        out_specs=[o_spec, o_spec],
        out_shape=[jax.ShapeDtypeStruct((e, LANES_V7X), i32), jax.ShapeDtypeStruct((e, LANES_V7X), f32)],
        compiler_params=pltpu.CompilerParams(vmem_limit_bytes=VMEM_LIMIT_V7X),
        name="route_threshold",
    )(aff_t)


RANK_BLOCK = 2048
NOT_ROUTED = -(1 << 30)


def _rank_kernel(aff_ref, thr_ref, need_ref, tri_ref, cum_ref, pos_ref, carry):
    @pl.when(pl.program_id(0) == 0)
    def _():
        carry[...] = jnp.zeros_like(carry)

    thr = thr_ref[:, :1]
    need = need_ref[:, :1]
    tri = tri_ref[...]
    c1 = carry[0][:, :1]
    c2 = carry[1][:, :1]
    for g in range(aff_ref.shape[1] // LANES_V7X):
        lanes = slice(g * LANES_V7X, (g + 1) * LANES_V7X)
        bits = pltpu.bitcast(aff_ref[:, lanes], i32)
        eq = bits == thr
        r1 = jnp.dot(jnp.where(eq, 1.0, 0.0).astype(bf16), tri, preferred_element_type=f32) + c1
        sel = (bits > thr) | (eq & (r1 <= need))
        r2 = jnp.dot(jnp.where(sel, 1.0, 0.0).astype(bf16), tri, preferred_element_type=f32) + c2
        r2i = r2.astype(i32)
        cum_ref[:, lanes] = r2i
        pos_ref[:, lanes] = jnp.where(sel, r2i - 1, NOT_ROUTED)
        c1 = r1[:, LANES_V7X - 1:]
        c2 = r2[:, LANES_V7X - 1:]
    carry[0] = jnp.broadcast_to(c1, carry.shape[1:])
    carry[1] = jnp.broadcast_to(c2, carry.shape[1:])


def _rank(aff_t, thr, need):
    e, n = aff_t.shape
    tb = _tile(n, RANK_BLOCK)
    r = jnp.arange(LANES_V7X)
    tri = (r[:, None] <= r[None, :]).astype(bf16)
    small = pl.BlockSpec((e, LANES_V7X), lambda i: (0, 0))
    blk = pl.BlockSpec((e, tb), lambda i: (0, i))
    return pl.pallas_call(
        _rank_kernel,
        grid=(n // tb,),
        in_specs=[blk, small, small, pl.BlockSpec((LANES_V7X, LANES_V7X), lambda i: (0, 0))],
        out_specs=[blk, blk],
        out_shape=[jax.ShapeDtypeStruct((e, n), i32), jax.ShapeDtypeStruct((e, n), i32)],
        scratch_shapes=[pltpu.VMEM((2, e, LANES_V7X), f32)],
        compiler_params=_cparams(("arbitrary",)),
        name="route_rank",
    )(aff_t, thr, need, tri)


def _compact_kernel(bc_ref, cum_ref, idx_ref):
    e = pl.program_id(0)
    cap = idx_ref.shape[2]
    L = LANES_V7X
    pcol0 = lax.broadcasted_iota(i32, (L, 1), 0)
    eye = lax.broadcasted_iota(i32, (L, L), 0) == lax.broadcasted_iota(i32, (L, L), 1)

    def tile(j, b_lo):
        p0 = j * L
        b_lo = lax.while_loop(lambda b: bc_ref[e, b] <= p0, lambda b: b + 1, b_lo)
        pcol = p0 + pcol0

        def add_group(b, acc):
            row = cum_ref[0, :, pl.ds(pl.multiple_of(b * L, L), L)]
            return acc + jnp.where(row <= pcol, 1.0, 0.0)

        def cond(c):
            return bc_ref[e, c[0] - 1] < p0 + L

        def body(c):
            return c[0] + 1, add_group(c[0], c[1])

        _, acc = lax.while_loop(cond, body, (b_lo + 1, add_group(b_lo, jnp.zeros((L, L), f32))))
        col = jnp.sum(acc, axis=1, keepdims=True) + (b_lo * L).astype(f32)
        row = jnp.sum(jnp.where(eye, col, 0.0), axis=0, keepdims=True)
        idx_ref[0, :, pl.ds(pl.multiple_of(p0, L), L)] = row.astype(i32)
        return b_lo

    lax.fori_loop(0, cap // L, tile, 0)


def _compact(cum, cap):
    e, n = cum.shape
    bc = cum[:, LANES_V7X - 1::LANES_V7X]
    idx = pl.pallas_call(
        _compact_kernel,
        grid=(e,),
        in_specs=[pl.BlockSpec(memory_space=pltpu.SMEM), pl.BlockSpec((1, 1, n), lambda ei: (ei, 0, 0))],
        out_specs=pl.BlockSpec((1, 1, cap), lambda ei: (ei, 0, 0)),
        out_shape=jax.ShapeDtypeStruct((e, 1, cap), i32),
        compiler_params=_cparams(("arbitrary",)),
        name="route_compact",
    )(bc, cum.reshape(e, 1, n))
    return idx.reshape(e, cap)


def _select(aff_t, cap):
    thr, need = _threshold(aff_t, cap)
    cum, pos = _rank(aff_t, thr, need)
    return _compact(cum, cap), cum, pos


def _ffn_kernel(idx_ref, idxn_ref, h2_hbm, wg_ref, wu_ref, wd_ref, ye_ref, xbuf, sem):
    tc = xbuf.shape[1]
    s = pl.program_id(0)
    last = pl.num_programs(0) - 1
    slot = s % 2

    def x_copy(ref, i, sl):
        return pltpu.make_async_copy(h2_hbm.at[pl.ds(ref[0, 0, i], 1)], xbuf.at[sl, pl.ds(i, 1)], sem.at[sl])

    def wait_tile(sl):
        pltpu.make_async_copy(h2_hbm.at[pl.ds(0, tc)], xbuf.at[sl], sem.at[sl]).wait()

    @pl.when(s == 0)
    def _():
        def body(i, c):
            x_copy(idx_ref, i, 0).start()
            return c
        lax.fori_loop(0, tc, body, 0, unroll=8)

    wait_tile(slot)
    half = D_MODEL // 2
    w = xbuf[slot]
    x_lo = pltpu.bitcast(w << 16, f32).astype(bf16)
    x_hi = pltpu.bitcast(w & jnp.uint32(0xFFFF0000), f32).astype(bf16)

    def proj(w_ref):
        return (jnp.dot(x_lo, w_ref[0, :half, :], preferred_element_type=f32)
                + jnp.dot(x_hi, w_ref[0, half:, :], preferred_element_type=f32))

    hg = proj(wg_ref)
    for i in range(tc // 2):
        x_copy(idxn_ref, i, 1 - slot).start(priority=i % 2)
    hu = proj(wu_ref)
    hid = (hg * _sigmoid(hg) * hu).astype(bf16)
    for i in range(tc // 2, tc):
        x_copy(idxn_ref, i, 1 - slot).start(priority=i % 2)
    ye_ref[...] = jnp.dot(hid, wd_ref[0], preferred_element_type=f32).astype(bf16)

    @pl.when(s == last)
    def _():
        wait_tile(1 - slot)


def _ffn(idx, h2, wg, wu, wd):
    e, cap = idx.shape
    tc = _tile(cap, 256)
    nt = cap // tc
    steps = e * nt
    wspec = pl.BlockSpec((1, D_MODEL, D_FF), lambda si: (si // nt, 0, 0))
    idx3 = idx.reshape(steps, 1, tc)
    return pl.pallas_call(
        _ffn_kernel,
        grid=(steps,),
        in_specs=[
            pl.BlockSpec((1, 1, tc), lambda si: (si, 0, 0), memory_space=pltpu.SMEM),
            pl.BlockSpec((1, 1, tc), lambda si: (jnp.minimum(si + 1, steps - 1), 0, 0), memory_space=pltpu.SMEM),
            pl.BlockSpec(memory_space=pl.ANY),
            wspec, wspec,
            pl.BlockSpec((1, D_FF, D_MODEL), lambda si: (si // nt, 0, 0)),
        ],
        out_specs=pl.BlockSpec((tc, D_MODEL), lambda si: (si, 0)),
        out_shape=jax.ShapeDtypeStruct((e * cap, D_MODEL), bf16),
        scratch_shapes=[pltpu.VMEM((2, tc, D_MODEL // 2), jnp.uint32), pltpu.SemaphoreType.DMA((2,))],
        compiler_params=_cparams(("arbitrary",)),
        name="expert_ffn",
    )(idx3, idx3, h2, wg, wu, wd)


COMB_CH = 32
COMB_KB = 256
COMB_ALIGN = 2 * SUBLANES_V7X
META_START, META_NCH, META_OFF, META_SLOT, META_TOT = 0, 16, 32, 48, 64
META_W = 128


def _combine_kernel(meta_ref, metan_ref, x1_ref, pos_ref, aff_ref, ye_hbm, g_ref, o_ref, slab, oht, acc, sem):
    tn = x1_ref.shape[0]
    i = pl.program_id(0)
    last = pl.num_programs(0) - 1
    buf = i % 2

    def pieces(mref, b, fn):
        for e in range(N_EXPERTS):
            def body(c, carry, e=e):
                src = ye_hbm.at[pl.ds(pl.multiple_of(mref[0, 0, META_START + e] + c * COMB_CH, COMB_ALIGN),
                                      COMB_CH)]
                dst = slab.at[b, pl.ds(pl.multiple_of(mref[0, 0, META_OFF + e] + c * COMB_CH, COMB_CH), COMB_CH)]
                fn(pltpu.make_async_copy(src, dst, sem.at[b]))
                return carry
            lax.fori_loop(0, mref[0, 0, META_NCH + e], body, 0)

    @pl.when(i == 0)
    def _():
        slab[...] = jnp.zeros_like(slab)
        pieces(meta_ref, 0, lambda cp: cp.start())

    @pl.when(i < last)
    def _():
        pieces(metan_ref, 1 - buf, lambda cp: cp.start())

    rowi = lax.broadcasted_iota(i32, (COMB_CH, 1), 0)
    for e in range(N_EXPERTS):
        pos_e = pos_ref[e:e + 1, :]
        gate_e = aff_ref[e:e + 1, :]

        def fill(c, carry, e=e, pos_e=pos_e, gate_e=gate_e):
            slot = meta_ref[0, 0, META_SLOT + e] + c * COMB_CH + rowi
            row0 = pl.multiple_of(meta_ref[0, 0, META_OFF + e] + c * COMB_CH, COMB_CH)
            oht[pl.ds(row0, COMB_CH), :] = jnp.where(pos_e == slot, gate_e, 0.0).astype(bf16)
            return carry
        lax.fori_loop(0, meta_ref[0, 0, META_NCH + e], fill, 0)

    ktot = meta_ref[0, 0, META_TOT]
    nkb = (ktot + COMB_KB - 1) // COMB_KB

    def clear(c, carry):
        row0 = pl.multiple_of(ktot + c * COMB_CH, COMB_CH)
        oht[pl.ds(row0, COMB_CH), :] = jnp.zeros((COMB_CH, tn), bf16)
        return carry
    lax.fori_loop(0, (nkb * COMB_KB - ktot) // COMB_CH, clear, 0)

    pieces(meta_ref, buf, lambda cp: cp.wait())
    acc[...] = x1_ref[...]

    def mm(j, carry):
        rows = pl.ds(pl.multiple_of(j * COMB_KB, COMB_KB), COMB_KB)
        acc[...] += lax.dot_general(oht[rows, :], slab[buf, rows, :], TN, preferred_element_type=f32)
        return carry
    lax.fori_loop(0, nkb, mm, 0)
    o_ref[...] = _rms(acc[...], g_ref[...])


def _combine(meta, x1, pos, aff_t, ye, g):
    n = x1.shape[0]
    tn = _tile(n, 256)
    nt = n // tn
    kmax = N_EXPERTS * (tn + 2 * COMB_CH)
    assert kmax % COMB_KB == 0
    return pl.pallas_call(
        _combine_kernel,
        grid=(nt,),
        in_specs=[
            pl.BlockSpec((1, 1, META_W), lambda i: (i, 0, 0), memory_space=pltpu.SMEM),
            pl.BlockSpec((1, 1, META_W), lambda i: (jnp.minimum(i + 1, nt - 1), 0, 0), memory_space=pltpu.SMEM),
            pl.BlockSpec((tn, D_MODEL), lambda i: (i, 0)),
            pl.BlockSpec((N_EXPERTS, tn), lambda i: (0, i)),
            pl.BlockSpec((N_EXPERTS, tn), lambda i: (0, i)),
            pl.BlockSpec(memory_space=pl.ANY),
            pl.BlockSpec((1, D_MODEL), lambda i: (0, 0)),
        ],
        out_specs=pl.BlockSpec((tn, D_MODEL), lambda i: (i, 0)),
        out_shape=jax.ShapeDtypeStruct((n, D_MODEL), f32),
        scratch_shapes=[pltpu.VMEM((2, kmax, D_MODEL), bf16), pltpu.VMEM((kmax, tn), bf16),
                        pltpu.VMEM((tn, D_MODEL), f32), pltpu.SemaphoreType.DMA((2,))],
        compiler_params=_cparams(("arbitrary",)),
        name="combine_norm",
    )(meta, meta, x1, pos, aff_t, ye, g)


def _combine_meta(cum, cap, tn):
    e, n = cum.shape
    nt = n // tn
    end = cum[:, tn - 1::tn]
    before = jnp.concatenate([jnp.zeros((e, 1), i32), end[:, :-1]], axis=1)
    base = jnp.arange(e, dtype=i32)[:, None] * cap
    start = (base + before) // COMB_ALIGN * COMB_ALIGN
    nch = jnp.where(end > before, (base + end - start + COMB_CH - 1) // COMB_CH, 0)
    start = jnp.minimum(start, e * cap - nch * COMB_CH)
    rows = nch * COMB_CH
    off = jnp.cumsum(rows, axis=0) - rows
    tot = jnp.sum(rows, axis=0, keepdims=True)
    pad = jnp.zeros((META_W - META_TOT - 1, nt), i32)
    meta = jnp.concatenate([start, nch, off, start - base, tot, pad], axis=0)
    return meta.T.reshape(nt, 1, META_W)


def _prepare_params(norm_mix, w_in, shift_mu_prev, shift_mu_next, decay_w0, decay_up, iclr_a0, iclr_up,
                    gate_up, k_k, k_a, r_k, lnx_w, lnx_b, w_o_rwkv, dw_kernel, dw_bias, conv_ln_w, conv_ln_b,
                    w_o_conv, w_out, norm_ffn, router_w, router_b, exp_w_gate, exp_w_up, exp_w_down, norm_final):
    l = 0
    rwkv_cols = RKV_COLS + LORA_COLS
    w = w_in[l]
    pad = jnp.zeros((D_MODEL, LORA_PAD - LORA_COLS), f32)
    wa = jnp.concatenate([w[:, :rwkv_cols], pad], axis=1).astype(bf16)
    wu = w[:, rwkv_cols:rwkv_cols + 2 * C_CONV].astype(bf16)
    wgate = w[:, rwkv_cols + 2 * C_CONV:].astype(bf16)
    padv = jnp.zeros((LORA_PAD - LORA_COLS,), f32)
    mup = jnp.concatenate([shift_mu_prev[l], padv])[None, :]
    mun = jnp.concatenate([shift_mu_next[l], padv])[None, :]
    wup = jnp.zeros((LORA_PAD, UP_COLS), f32)
    wup = wup.at[0:W_LORA, 0:D_RWKV].set(decay_up[l, 0])
    wup = wup.at[W_LORA:2 * W_LORA, D_RWKV:2 * D_RWKV].set(decay_up[l, 1])
    o = 2 * W_LORA
    wup = wup.at[o:o + A_LORA, 2 * D_RWKV:3 * D_RWKV].set(iclr_up[l, 0])
    wup = wup.at[o + A_LORA:o + 2 * A_LORA, 3 * D_RWKV:4 * D_RWKV].set(iclr_up[l, 1])
    o = 2 * W_LORA + 2 * A_LORA
    wup = wup.at[o:o + G_LORA, 4 * D_RWKV:5 * D_RWKV].set(gate_up[l])
    head = jnp.arange(D_RWKV) // HEAD
    same = (head[:, None] == head[None, :])
    return dict(
        norm_mix=norm_mix[l][None, :], wa=wa, wu=wu, wgate=wgate, mup=mup, mun=mun, wup=wup.astype(bf16),
        w0=decay_w0[l], a0=iclr_a0[l], kkw=k_k[l][None, :], ka=k_a[l][None, :], rk=r_k[l].reshape(1, D_RWKV),
        hsum=same.astype(bf16), havg=(same.astype(f32) / HEAD).astype(bf16),
        lnxw=lnx_w[l][None, :], lnxb=lnx_b[l][None, :], wor=w_o_rwkv[l].astype(bf16),
        kern=dw_kernel[l], cbias=dw_bias[l][None, :], clnw=conv_ln_w[l][None, :], clnb=conv_ln_b[l][None, :],
        woc=w_o_conv[l].astype(bf16), wout=w_out[l].astype(bf16), nffn=norm_ffn[l][None, :],
        rwt=router_w[l].T.astype(bf16), rb=router_b[l][:, None],
        ewg=exp_w_gate[l].astype(bf16), ewu=exp_w_up[l].astype(bf16), ewd=exp_w_down[l].astype(bf16),
        nfinal=norm_final[None, :],
    )


def _trunk(x, P):
    b, t, d = x.shape
    n = b * t
    x2 = x.reshape(n, d)
    p, u = _inproj(x2, P["norm_mix"], P["wa"], P["wu"])
    r, v, kk, ew0, ew1, k0, k1, b0, b1, g, bonus = _prep(
        p.reshape(b, t, PA_COLS), P["mup"], P["mun"], P["wup"], P["w0"], P["a0"], P["kkw"], P["ka"], P["rk"],
        P["hsum"])
    yf, yb = _scan(r, v, kk, ew0, ew1, k0, k1, b0, b1)
    flat = lambda a: a.reshape(n, a.shape[-1])
    x1, h2, aff_t = _merge(t, x2, flat(yf), flat(yb), flat(g), flat(bonus), u, P["kern"], P["cbias"],
                           P["clnw"], P["clnb"], P["norm_mix"], P["wgate"], P["lnxw"], P["lnxb"], P["havg"],
                           P["wor"], P["woc"], P["wout"], P["nffn"], P["rwt"], P["rb"])
    cap = max(1, CAPACITY_FACTOR * n // N_EXPERTS)
    idx, cum, pos = _select(aff_t, cap)
    ye = _ffn(idx, h2, P["ewg"], P["ewu"], P["ewd"])
    meta = _combine_meta(cum, cap, _tile(n, 256))
    return _combine(meta, x1, pos, aff_t, ye, P["nfinal"]).reshape(b, t, d)


def kernel(x_prompt, x_sample, norm_mix, w_in, shift_mu_prev, shift_mu_next, decay_w0, decay_up, iclr_a0, iclr_up, gate_up, k_k, k_a, r_k, lnx_w, lnx_b, w_o_rwkv, dw_kernel, dw_bias, conv_ln_w, conv_ln_b, w_o_conv, w_out, norm_ffn, router_w, router_b, exp_w_gate, exp_w_up, exp_w_down, norm_final):
    P = _prepare_params(norm_mix, w_in, shift_mu_prev, shift_mu_next, decay_w0, decay_up, iclr_a0, iclr_up,
                        gate_up, k_k, k_a, r_k, lnx_w, lnx_b, w_o_rwkv, dw_kernel, dw_bias, conv_ln_w,
                        conv_ln_b, w_o_conv, w_out, norm_ffn, router_w, router_b, exp_w_gate, exp_w_up,
                        exp_w_down, norm_final)
    return (_trunk(x_prompt, P), _trunk(x_sample, P))
```

```python
import functools
import math

import jax
import jax.numpy as jnp
from jax import lax
from jax.experimental import pallas as pl
from jax.experimental.pallas import tpu as pltpu

f32 = jnp.float32
bf16 = jnp.bfloat16
i32 = jnp.int32

D_MODEL = 1024
D_RWKV = 512
HEAD = 64
W_LORA = 32
A_LORA = 32
G_LORA = 96
C_CONV = 512
CONV_W = 31
N_EXPERTS = 16
D_FF = 1024
CAPACITY_FACTOR = 2
GN_EPS = 64e-5
LN_EPS = 1e-5
RMS_EPS = 1e-6
LORA_COLS = 2 * W_LORA + 2 * A_LORA + G_LORA
LORA_PAD = 256
RKV_COLS = 3 * D_RWKV
PA_COLS = RKV_COLS + LORA_PAD
UP_COLS = 5 * D_RWKV

LANES_V7X = 128
SUBLANES_V7X = 8
MXU_DIM_V7X = 256
VMEM_LIMIT_V7X = 56 * 1024 * 1024

CHUNK = 64
SCAN_UNROLL = 4
SCAN_BLOCK = 512
GROUP = MXU_DIM_V7X // HEAD
GROUP_LANES = GROUP * HEAD
N_GROUPS = D_RWKV // GROUP_LANES

NN = (((1,), (0,)), ((), ()))
NT = (((1,), (1,)), ((), ()))
TN = (((0,), (0,)), ((), ()))


def _mm(a, b, dims=NN):
    return lax.dot_general(a.astype(bf16), b.astype(bf16), dims, preferred_element_type=f32)


def _mm_split(a, x, n):
    out = None
    rem = x
    for _ in range(n):
        hi = rem.astype(bf16)
        t = lax.dot_general(a, hi, NN, preferred_element_type=f32)
        out = t if out is None else out + t
        rem = rem - hi.astype(f32)
    return out


def _mm_split_lhs(x, a, n):
    out = None
    rem = x
    for _ in range(n):
        hi = rem.astype(bf16)
        t = lax.dot_general(hi, a, NN, preferred_element_type=f32)
        out = t if out is None else out + t
        rem = rem - hi.astype(f32)
    return out


def _sigmoid(x):
    return 1.0 / (1.0 + jnp.exp(-x))


def _rms(x, g):
    return x * lax.rsqrt(jnp.mean(x * x, axis=-1, keepdims=True) + RMS_EPS) * g


def _cparams(sem):
    return pltpu.CompilerParams(dimension_semantics=sem, vmem_limit_bytes=VMEM_LIMIT_V7X)


def _tile(n, pref):
    t = min(n, pref)
    assert n % t == 0, (n, pref)
    return t


def _inproj_kernel(x_ref, g_ref, wa_ref, wu_ref, p_ref, u_ref):
    h = _rms(x_ref[...], g_ref[...]).astype(bf16)
    p_ref[...] = jnp.dot(h, wa_ref[...], preferred_element_type=f32).astype(bf16)
    u_ref[...] = jnp.dot(h, wu_ref[...], preferred_element_type=f32).astype(bf16)


def _inproj(x2, g, wa, wu):
    n = x2.shape[0]
    tm = _tile(n, 512)
    return pl.pallas_call(
        _inproj_kernel,
        grid=(n // tm,),
        in_specs=[
            pl.BlockSpec((tm, D_MODEL), lambda i: (i, 0)),
            pl.BlockSpec((1, D_MODEL), lambda i: (0, 0)),
            pl.BlockSpec((D_MODEL, PA_COLS), lambda i: (0, 0)),
            pl.BlockSpec((D_MODEL, 2 * C_CONV), lambda i: (0, 0)),
        ],
        out_specs=[
            pl.BlockSpec((tm, PA_COLS), lambda i: (i, 0)),
            pl.BlockSpec((tm, 2 * C_CONV), lambda i: (i, 0)),
        ],
        out_shape=[
            jax.ShapeDtypeStruct((n, PA_COLS), bf16),
            jax.ShapeDtypeStruct((n, 2 * C_CONV), bf16),
        ],
        compiler_params=_cparams(("parallel",)),
        name="inproj",
    )(x2, g, wa, wu)


PREP_HALO = 16


def _prep_kernel(pc_ref, pp_ref, pn_ref, mup_ref, mun_ref, wup_ref, w0_ref, a0_ref, kkw_ref, ka_ref,
                 rk_ref, hsum_ref, r_o, v_o, kk_o, ew0_o, ew1_o, k0_o, k1_o, b0_o, b1_o, g_o, bonus_o):
    tt = pc_ref.shape[1]
    i = pl.program_id(1)
    nt = pl.num_programs(1)
    row = lax.broadcasted_iota(i32, (tt, 1), 0)
    first = i == 0
    last = i == nt - 1

    def shifted(c0, c1):
        p = pc_ref[0, :, c0:c1].astype(f32)
        prev_row = jnp.where(first, 0.0, pp_ref[0, PREP_HALO - 1:PREP_HALO, c0:c1].astype(f32))
        next_row = jnp.where(last, 0.0, pn_ref[0, 0:1, c0:c1].astype(f32))
        zp = jnp.where(row == 0, prev_row, pltpu.roll(p, 1, axis=0))
        zn = jnp.where(row == tt - 1, next_row, pltpu.roll(p, tt - 1, axis=0))
        return p + mup_ref[:, c0:c1] * (zp - p) + mun_ref[:, c0:c1] * (zn - p)

    r = shifted(0, D_RWKV)
    k = shifted(D_RWKV, 2 * D_RWKV)
    v = shifted(2 * D_RWKV, 3 * D_RWKV)
    zl = shifted(RKV_COLS, PA_COLS)

    lane = lax.broadcasted_iota(i32, (1, LORA_PAD), 1)
    act = jnp.where(lane < 2 * W_LORA, jnp.tanh(zl),
                    jnp.where(lane < 2 * W_LORA + 2 * A_LORA, zl, _sigmoid(zl)))
    pre = jnp.dot(act.astype(bf16), wup_ref[...], preferred_element_type=f32)

    hsum = hsum_ref[...]
    kkr = k * kkw_ref[...]
    ss = _mm_split_lhs(kkr * kkr, hsum, 2)
    kk = kkr / jnp.maximum(jnp.sqrt(ss), 1e-12)
    ka = ka_ref[...]
    ksum = None
    outs = ((ew0_o, k0_o, b0_o), (ew1_o, k1_o, b1_o))
    for d in range(2):
        x = pre[:, d * D_RWKV:(d + 1) * D_RWKV] + w0_ref[d:d + 1, :]
        ew = math.exp(-0.5) * _sigmoid(x)
        alpha = _sigmoid(pre[:, (2 + d) * D_RWKV:(3 + d) * D_RWKV] + a0_ref[d:d + 1, :])
        kd = k * (1.0 + (alpha - 1.0) * ka)
        ew_o, k_o, b_o = outs[d]
        ew_o[0] = ew
        k_o[0] = kd.astype(bf16)
        b_o[0] = (kk * alpha).astype(bf16)
        ksum = kd if ksum is None else ksum + kd
    bonus = _mm_split_lhs(r * ksum * rk_ref[...], hsum, 2) * v
    r_o[0] = r.astype(bf16)
    v_o[0] = v.astype(bf16)
    kk_o[0] = kk.astype(bf16)
    g_o[0] = pre[:, 4 * D_RWKV:5 * D_RWKV].astype(bf16)
    bonus_o[0] = bonus.astype(bf16)


def _prep(p3, mup, mun, wup, w0, a0, kkw, ka, rk, hsum):
    b, t, _ = p3.shape
    tt = _tile(t, 256)
    nt = t // tt
    hb = tt // PREP_HALO
    full = lambda s: pl.BlockSpec(s, lambda bi, ti: tuple(0 for _ in s))
    o_spec = pl.BlockSpec((1, tt, D_RWKV), lambda bi, ti: (bi, ti, 0))
    o_lo = jax.ShapeDtypeStruct((b, t, D_RWKV), bf16)
    o_hi = jax.ShapeDtypeStruct((b, t, D_RWKV), f32)
    return pl.pallas_call(
        _prep_kernel,
        grid=(b, nt),
        in_specs=[
            pl.BlockSpec((1, tt, PA_COLS), lambda bi, ti: (bi, ti, 0)),
            pl.BlockSpec((1, PREP_HALO, PA_COLS), lambda bi, ti: (bi, jnp.maximum(ti * hb - 1, 0), 0)),
            pl.BlockSpec((1, PREP_HALO, PA_COLS),
                         lambda bi, ti: (bi, jnp.minimum((ti + 1) * hb, t // PREP_HALO - 1), 0)),
            full((1, PA_COLS)), full((1, PA_COLS)), full((LORA_PAD, UP_COLS)),
            full((2, D_RWKV)), full((2, D_RWKV)), full((1, D_RWKV)), full((1, D_RWKV)), full((1, D_RWKV)),
            full((D_RWKV, D_RWKV)),
        ],
        out_specs=[o_spec] * 11,
        out_shape=[o_lo, o_lo, o_lo, o_hi, o_hi] + [o_lo] * 6,
        compiler_params=_cparams(("parallel", "parallel")),
        name="rwkv_prep",
    )(p3, p3, p3, mup, mun, wup, w0, a0, kkw, ka, rk, hsum)


def _scan_consts(reverse):
    L = CHUNK
    t = lax.broadcasted_iota(i32, (L, GROUP * L), 0)
    s = lax.broadcasted_iota(i32, (L, GROUP * L), 1) & (L - 1)
    strict = (s > t) if reverse else (s < t)
    eye = s == t
    incl = strict | eye
    tt = lax.broadcasted_iota(i32, (L, L), 0)
    ss = lax.broadcasted_iota(i32, (L, L), 1)
    tri = jnp.where((ss >= tt) if reverse else (ss <= tt), 1.0, 0.0).astype(bf16)
    rr = lax.broadcasted_iota(i32, (GROUP_LANES, GROUP_LANES), 0) // HEAD
    cc = lax.broadcasted_iota(i32, (GROUP_LANES, GROUP_LANES), 1) // HEAD
    same = rr == cc
    return strict, incl, eye, tri, (same, jnp.where(same, 1.0, 0.0).astype(bf16))


def _bd(x, bdmask):
    xb = x.astype(bf16)
    return jnp.concatenate([xb] * GROUP, axis=0) * bdmask[1]


def _wkv_chunks(chains):
    L = CHUNK
    n = len(chains)
    rng = range(n)
    r, v, kk, ew, k, b, consts, rev = (list(c) for c in zip(*chains))
    strict = [c[0] for c in consts]
    incl = [c[1] for c in consts]
    eye = [c[2] for c in consts]
    tri = [c[3] for c in consts]
    bdm = consts[0][4]
    ginc = [-_mm_split(tri[i], ew[i], 3) for i in rng]
    gref = [0.5 * (ginc[i][0:1] if rev[i] else ginc[i][L - 1:L]) for i in rng]
    egref = [jnp.exp(gref[i]) for i in rng]
    e1 = [jnp.exp(ginc[i] - gref[i]) for i in rng]
    e2 = [jnp.exp(gref[i] - ginc[i]) for i in rng]
    ea = [jnp.exp(ginc[i] + ew[i] - gref[i]) for i in rng]
    At = [-kk[i] * ea[i] for i in rng]
    Rt = [r[i] * e1[i] for i in rng]
    Kt = [k[i] * e2[i] for i in rng]
    Bt = [b[i] * e2[i] for i in rng]
    Be = [Bt[i] * egref[i] for i in rng]
    AR = [jnp.concatenate([At[i], Rt[i]], axis=0) for i in rng]
    pb = [_mm(AR[i], _bd(Bt[i], bdm), NT) for i in rng]
    pk = [_mm(AR[i], _bd(Kt[i], bdm), NT) for i in rng]
    Mab = [jnp.where(strict[i], pb[i][:L], 0.0) for i in rng]
    Arb = [jnp.where(incl[i], pb[i][L:], 0.0) for i in rng]
    Mak = [jnp.where(strict[i], pk[i][:L], 0.0) for i in rng]
    Ark = [jnp.where(incl[i], pk[i][L:], 0.0) for i in rng]
    vbd = [_bd(v[i], bdm) for i in rng]
    MakV = [_mm(Mak[i], vbd[i]) for i in rng]
    X = [jnp.where(eye[i], 1.0, Mab[i]) for i in rng]
    P = [_mm(Mab[i], _bd(Mab[i], bdm)) for i in rng]
    for step in range(4):
        XP = [_mm(jnp.concatenate([X[i], P[i]], axis=0), _bd(P[i], bdm)) for i in rng]
        X = [X[i] + XP[i][:L] for i in rng]
        P = [XP[i][L:] for i in rng]
    X = [X[i] + _mm(X[i], _bd(P[i], bdm)) for i in rng]
    W1 = [_mm(X[i], _bd(At[i] * egref[i], bdm)) for i in rng]
    W2 = [_mm(X[i], _bd(MakV[i], bdm)) for i in rng]
    Q = [Rt[i] * egref[i] + _mm(Arb[i], _bd(W1[i], bdm)) for i in rng]
    Y0 = [_mm(Arb[i], _bd(W2[i], bdm)) + _mm(Ark[i], vbd[i]) for i in rng]
    Pm = [jnp.where(bdm[0], _mm(W1[i], Be[i], TN), 0.0) for i in rng]
    Qm = [jnp.where(bdm[0], _mm(jnp.concatenate([W2[i], v[i]], axis=0),
                                jnp.concatenate([Be[i], Kt[i] * egref[i]], axis=0), TN), 0.0) for i in rng]
    D = [egref[i] * egref[i] for i in rng]
    return Q, Y0, Pm, Qm, D


def _scan_kernel(rf, vf, kkf, ewf, kf, bf, rb, vb, kkb, ewb, kb, bb, yf_ref, yb_ref, s_ref):
    tb = rf.shape[1]
    nch = tb // CHUNK

    @pl.when(pl.program_id(1) == 0)
    def _():
        s_ref[...] = jnp.zeros_like(s_ref)

    consts_f = _scan_consts(False)
    consts_b = _scan_consts(True)

    def body(j, carry):
        items, dests = [], []
        for sub in range(SCAN_UNROLL):
            for reverse, refs, y_ref, consts in ((False, (rf, vf, kkf, ewf, kf, bf), yf_ref, consts_f),
                                                 (True, (rb, vb, kkb, ewb, kb, bb), yb_ref, consts_b)):
                step = j * SCAN_UNROLL + sub
                cj = (nch - 1 - step) if reverse else step
                rows = pl.ds(pl.multiple_of(cj * CHUNK, CHUNK), CHUNK)
                for g in range(N_GROUPS):
                    lanes = slice(g * GROUP_LANES, (g + 1) * GROUP_LANES)
                    items.append(tuple(ref[0, rows, lanes].astype(f32) for ref in refs) + (consts, reverse))
                    dests.append((y_ref, rows, lanes, (2 if reverse else 0) + g))
        Q, Y0, Pm, Qm, D = _wkv_chunks(items)
        S = [s_ref[si] for si in range(2 * N_GROUPS)]
        for i, (y_ref, rows, lanes, si) in enumerate(dests):
            y_ref[0, rows, lanes] = _mm(Q[i], S[si], NT) + Y0[i]
            S[si] = S[si] * D[i] + _mm(S[si], Pm[i]) + Qm[i]
        for si in range(2 * N_GROUPS):
            s_ref[si] = S[si]
        return carry

    lax.fori_loop(0, nch // SCAN_UNROLL, body, 0)


def _scan(r, v, kk, ew0, ew1, k0, k1, b0, b1):
    b, t, _ = r.shape
    tb = _tile(t, SCAN_BLOCK)
    assert tb % (CHUNK * SCAN_UNROLL) == 0
    nb = t // tb
    fwd = pl.BlockSpec((1, tb, D_RWKV), lambda bi, ti: (bi, ti, 0))
    bwd = pl.BlockSpec((1, tb, D_RWKV), lambda bi, ti: (bi, nb - 1 - ti, 0))
    o_shape = jax.ShapeDtypeStruct((b, t, D_RWKV), f32)
    return pl.pallas_call(
        _scan_kernel,
        grid=(b, nb),
        in_specs=[fwd] * 6 + [bwd] * 6,
        out_specs=[fwd, bwd],
        out_shape=[o_shape, o_shape],
        scratch_shapes=[pltpu.VMEM((2 * N_GROUPS, GROUP_LANES, GROUP_LANES), f32)],
        compiler_params=_cparams(("parallel", "arbitrary")),
        name="wkv_scan",
    )(r, v, kk, ew0, k0, b0, r, v, kk, ew1, k1, b1)


CONV_HALO = 16
CONV_ROWS = 32
MERGE_SPLIT = 2


def _conv_tile(uc_ref, up_ref, un_ref, kern_ref, bias_ref, lnw_ref, lnb_ref, ext_ref, sh_ref, first, last):
    tt = uc_ref.shape[0]

    def glu(u):
        u = u.astype(f32)
        return u[:, :C_CONV] * _sigmoid(u[:, C_CONV:])

    ext_ref[0:CONV_HALO, :] = jnp.where(first, 0.0, glu(up_ref[...]))
    ext_ref[CONV_HALO:CONV_HALO + tt, :] = glu(uc_ref[...])
    ext_ref[CONV_HALO + tt:, :] = jnp.where(last, 0.0, glu(un_ref[...]))
    span = sh_ref.shape[1]
    for r in range(1, SUBLANES_V7X):
        sh_ref[r] = ext_ref[r:r + span, :]
    half = CONV_W // 2
    kern = [kern_ref[j:j + 1, :] for j in range(CONV_W)]
    outs = []
    for r0 in range(0, tt, CONV_ROWS):
        acc = jnp.zeros((CONV_ROWS, C_CONV), f32) + bias_ref[...]
        for j in range(CONV_W):
            off = CONV_HALO - half + j
            r, base = off % SUBLANES_V7X, off - off % SUBLANES_V7X + r0
            tap = ext_ref[base:base + CONV_ROWS, :] if r == 0 else sh_ref[r, base:base + CONV_ROWS, :]
            acc = acc + kern[j] * tap
        mu = jnp.mean(acc, axis=-1, keepdims=True)
        xc = acc - mu
        var = jnp.mean(xc * xc, axis=-1, keepdims=True)
        y = xc * lax.rsqrt(var + LN_EPS) * lnw_ref[...] + lnb_ref[...]
        outs.append(y * _sigmoid(y))
    return jnp.concatenate(outs, axis=0)


def _merge_kernel(x_ref, yf_ref, yb_ref, g_ref, bonus_ref, uc_ref, up_ref, un_ref, kern_ref, cbias_ref,
                  clnw_ref, clnb_ref, nmix_ref, wg_ref, lnxw_ref, lnxb_ref,
                  havg_ref, wor_ref, woc_ref, wout_ref, nffn_ref, rwt_ref, rb_ref,
                  x1_ref, h2_ref, aff_ref, ext_ref, sh_ref, *, tiles_per_seq):
    ti = pl.program_id(0) % tiles_per_seq
    tm = x_ref.shape[0]
    cb = _conv_tile(uc_ref, up_ref, un_ref, kern_ref, cbias_ref, clnw_ref, clnb_ref, ext_ref, sh_ref,
                    ti == 0, ti == tiles_per_seq - 1)
    rs = [slice(i * tm // MERGE_SPLIT, (i + 1) * tm // MERGE_SPLIT) for i in range(MERGE_SPLIT)]
    havg = havg_ref[...]
    half = D_MODEL // 2
    x = [x_ref[r, :] for r in rs]
    h = [_rms(xi, nmix_ref[...]).astype(bf16) for xi in x]
    wkv = [yf_ref[r, :] + yb_ref[r, :] for r in rs]
    mu = [_mm_split_lhs(w, havg, 2) for w in wkv]
    gates = [_sigmoid(jnp.dot(hi, wg_ref[...], preferred_element_type=f32)) for hi in h]
    xc = [w - m for w, m in zip(wkv, mu)]
    var = [_mm_split_lhs(c * c, havg, 2) for c in xc]
    y = [(c * lax.rsqrt(v + GN_EPS) * lnxw_ref[...] + lnxb_ref[...] + bonus_ref[r, :]) * g_ref[r, :]
         for c, v, r in zip(xc, var, rs)]
    y_a = [jnp.dot(yi.astype(bf16), wor_ref[...], preferred_element_type=f32) for yi in y]
    y_b = [jnp.dot(cb[r, :].astype(bf16), woc_ref[...], preferred_element_type=f32) for r in rs]
    m = [gi[:, :D_MODEL] * a + gi[:, D_MODEL:] * b for gi, a, b in zip(gates, y_a, y_b)]
    x1 = [xi + jnp.dot(mi.astype(bf16), wout_ref[...], preferred_element_type=f32) for xi, mi in zip(x, m)]
    h2 = [_rms(xi, nffn_ref[...]).astype(bf16) for xi in x1]
    for r, x1i, h2i in zip(rs, x1, h2):
        x1_ref[r, :] = x1i
        hbits = pltpu.bitcast(h2i.astype(f32), jnp.uint32)
        h2_ref[r, :] = (hbits[:, :half] >> 16) | (hbits[:, half:] & jnp.uint32(0xFFFF0000))
        logits = lax.dot_general(rwt_ref[...], h2i, NT, preferred_element_type=f32) + rb_ref[...]
        mx = jnp.max(logits, axis=0, keepdims=True)
        e = jnp.exp(logits - mx)
        aff_ref[:, r] = e / jnp.sum(e, axis=0, keepdims=True)


def _merge(t, x2, yf, yb, g, bonus, u2, kern, cbias, clnw, clnb, nmix, wg, lnxw, lnxb, havg, wor, woc, wout,
           nffn, rwt, rb):
    n = x2.shape[0]
    tm = _tile(t, 512)
    hb = tm // CONV_HALO
    row = lambda c: pl.BlockSpec((tm, c), lambda i: (i, 0))
    full = lambda s: pl.BlockSpec(s, lambda i: tuple(0 for _ in s))
    return pl.pallas_call(
        functools.partial(_merge_kernel, tiles_per_seq=t // tm),
        grid=(n // tm,),
        in_specs=[
            row(D_MODEL), row(D_RWKV), row(D_RWKV), row(D_RWKV), row(D_RWKV), row(2 * C_CONV),
            pl.BlockSpec((CONV_HALO, 2 * C_CONV), lambda i: (jnp.maximum(i * hb - 1, 0), 0)),
            pl.BlockSpec((CONV_HALO, 2 * C_CONV), lambda i: (jnp.minimum((i + 1) * hb, n // CONV_HALO - 1), 0)),
            full((CONV_W, C_CONV)), full((1, C_CONV)), full((1, C_CONV)), full((1, C_CONV)),
            full((1, D_MODEL)), full((D_MODEL, 2 * D_MODEL)), full((1, D_RWKV)), full((1, D_RWKV)),
            full((D_RWKV, D_RWKV)), full((D_RWKV, D_MODEL)), full((C_CONV, D_MODEL)),
            full((D_MODEL, D_MODEL)), full((1, D_MODEL)), full((N_EXPERTS, D_MODEL)), full((N_EXPERTS, 1)),
        ],
        out_specs=[row(D_MODEL), row(D_MODEL // 2), pl.BlockSpec((N_EXPERTS, tm), lambda i: (0, i))],
        out_shape=[
            jax.ShapeDtypeStruct((n, D_MODEL), f32),
            jax.ShapeDtypeStruct((n, D_MODEL // 2), jnp.uint32),
            jax.ShapeDtypeStruct((N_EXPERTS, n), f32),
        ],
        scratch_shapes=[pltpu.VMEM((tm + 2 * CONV_HALO, C_CONV), f32),
                        pltpu.VMEM((SUBLANES_V7X, tm + 2 * CONV_HALO - SUBLANES_V7X, C_CONV), f32)],
        compiler_params=_cparams(("parallel",)),
        name="merge_router",
    )(x2, yf, yb, g, bonus, u2, u2, u2, kern, cbias, clnw, clnb, nmix, wg, lnxw, lnxb, havg, wor, woc, wout,
      nffn, rwt, rb)


def _thr_kernel(aff_ref, thr_ref, need_ref, *, cap):
    def count_ge(mid):
        bits = pltpu.bitcast(aff_ref[...], i32)
        return jnp.sum(jnp.where(bits >= mid, 1.0, 0.0), axis=1, keepdims=True)

    def body(_, c):
        lo, hi = c
        mid = lo + ((hi - lo + 1) >> 1)
        ok = count_ge(mid) >= cap
        return jnp.where(ok, mid, lo), jnp.where(ok, hi, mid - 1)

    lo0 = jnp.zeros((N_EXPERTS, 1), i32)
    hi0 = jnp.full((N_EXPERTS, 1), 0x3F800000, i32)
    lo, _ = lax.fori_loop(0, 31, body, (lo0, hi0))
    thr_ref[...] = jnp.broadcast_to(lo, thr_ref.shape)
    bits = pltpu.bitcast(aff_ref[...], i32)
    need = cap - jnp.sum(jnp.where(bits > lo, 1.0, 0.0), axis=1, keepdims=True)
    need_ref[...] = jnp.broadcast_to(need, need_ref.shape)


def _threshold(aff_t, cap):
    e, n = aff_t.shape
    o_spec = pl.BlockSpec((e, LANES_V7X), lambda: (0, 0))
    return pl.pallas_call(
        functools.partial(_thr_kernel, cap=float(cap)),
        in_specs=[pl.BlockSpec((e, n), lambda: (0, 0))],
        out_specs=[o_spec, o_spec],
        out_shape=[jax.ShapeDtypeStruct((e, LANES_V7X), i32), jax.ShapeDtypeStruct((e, LANES_V7X), f32)],
        compiler_params=pltpu.CompilerParams(vmem_limit_bytes=VMEM_LIMIT_V7X),
        name="route_threshold",
    )(aff_t)


RANK_BLOCK = 2048
NOT_ROUTED = -(1 << 30)


def _rank_kernel(aff_ref, thr_ref, need_ref, tri_ref, cum_ref, pos_ref, carry):
    @pl.when(pl.program_id(0) == 0)
    def _():
        carry[...] = jnp.zeros_like(carry)

    thr = thr_ref[:, :1]
    need = need_ref[:, :1]
    tri = tri_ref[...]
    c1 = carry[0][:, :1]
    c2 = carry[1][:, :1]
    for g in range(aff_ref.shape[1] // LANES_V7X):
        lanes = slice(g * LANES_V7X, (g + 1) * LANES_V7X)
        bits = pltpu.bitcast(aff_ref[:, lanes], i32)
        eq = bits == thr
        r1 = jnp.dot(jnp.where(eq, 1.0, 0.0).astype(bf16), tri, preferred_element_type=f32) + c1
        sel = (bits > thr) | (eq & (r1 <= need))
        r2 = jnp.dot(jnp.where(sel, 1.0, 0.0).astype(bf16), tri, preferred_element_type=f32) + c2
        r2i = r2.astype(i32)
        cum_ref[:, lanes] = r2i
        pos_ref[:, lanes] = jnp.where(sel, r2i - 1, NOT_ROUTED)
        c1 = r1[:, LANES_V7X - 1:]
        c2 = r2[:, LANES_V7X - 1:]
    carry[0] = jnp.broadcast_to(c1, carry.shape[1:])
    carry[1] = jnp.broadcast_to(c2, carry.shape[1:])


def _rank(aff_t, thr, need):
    e, n = aff_t.shape
    tb = _tile(n, RANK_BLOCK)
    r = jnp.arange(LANES_V7X)
    tri = (r[:, None] <= r[None, :]).astype(bf16)
    small = pl.BlockSpec((e, LANES_V7X), lambda i: (0, 0))
    blk = pl.BlockSpec((e, tb), lambda i: (0, i))
    return pl.pallas_call(
        _rank_kernel,
        grid=(n // tb,),
        in_specs=[blk, small, small, pl.BlockSpec((LANES_V7X, LANES_V7X), lambda i: (0, 0))],
        out_specs=[blk, blk],
        out_shape=[jax.ShapeDtypeStruct((e, n), i32), jax.ShapeDtypeStruct((e, n), i32)],
        scratch_shapes=[pltpu.VMEM((2, e, LANES_V7X), f32)],
        compiler_params=_cparams(("arbitrary",)),
        name="route_rank",
    )(aff_t, thr, need, tri)


def _compact_kernel(bc_ref, cum_ref, idx_ref):
    e = pl.program_id(0)
    cap = idx_ref.shape[2]
    L = LANES_V7X
    pcol0 = lax.broadcasted_iota(i32, (L, 1), 0)
    eye = lax.broadcasted_iota(i32, (L, L), 0) == lax.broadcasted_iota(i32, (L, L), 1)

    def tile(j, b_lo):
        p0 = j * L
        b_lo = lax.while_loop(lambda b: bc_ref[e, b] <= p0, lambda b: b + 1, b_lo)
        pcol = p0 + pcol0

        def add_group(b, acc):
            row = cum_ref[0, :, pl.ds(pl.multiple_of(b * L, L), L)]
            return acc + jnp.where(row <= pcol, 1.0, 0.0)

        def cond(c):
            return bc_ref[e, c[0] - 1] < p0 + L

        def body(c):
            return c[0] + 1, add_group(c[0], c[1])

        _, acc = lax.while_loop(cond, body, (b_lo + 1, add_group(b_lo, jnp.zeros((L, L), f32))))
        col = jnp.sum(acc, axis=1, keepdims=True) + (b_lo * L).astype(f32)
        row = jnp.sum(jnp.where(eye, col, 0.0), axis=0, keepdims=True)
        idx_ref[0, :, pl.ds(pl.multiple_of(p0, L), L)] = row.astype(i32)
        return b_lo

    lax.fori_loop(0, cap // L, tile, 0)


def _compact(cum, cap):
    e, n = cum.shape
    bc = cum[:, LANES_V7X - 1::LANES_V7X]
    idx = pl.pallas_call(
        _compact_kernel,
        grid=(e,),
        in_specs=[pl.BlockSpec(memory_space=pltpu.SMEM), pl.BlockSpec((1, 1, n), lambda ei: (ei, 0, 0))],
        out_specs=pl.BlockSpec((1, 1, cap), lambda ei: (ei, 0, 0)),
        out_shape=jax.ShapeDtypeStruct((e, 1, cap), i32),
        compiler_params=_cparams(("arbitrary",)),
        name="route_compact",
    )(bc, cum.reshape(e, 1, n))
    return idx.reshape(e, cap)


def _select(aff_t, cap):
    thr, need = _threshold(aff_t, cap)
    cum, pos = _rank(aff_t, thr, need)
    return _compact(cum, cap), cum, pos


def _ffn_kernel(idx_ref, idxn_ref, h2_hbm, wg_ref, wu_ref, wd_ref, ye_ref, xbuf, sem):
    tc = xbuf.shape[1]
    s = pl.program_id(0)
    last = pl.num_programs(0) - 1
    slot = s % 2

    def x_copy(ref, i, sl):
        return pltpu.make_async_copy(h2_hbm.at[pl.ds(ref[0, 0, i], 1)], xbuf.at[sl, pl.ds(i, 1)], sem.at[sl])

    def wait_tile(sl):
        pltpu.make_async_copy(h2_hbm.at[pl.ds(0, tc)], xbuf.at[sl], sem.at[sl]).wait()

    @pl.when(s == 0)
    def _():
        def body(i, c):
            x_copy(idx_ref, i, 0).start()
            return c
        lax.fori_loop(0, tc, body, 0, unroll=8)

    wait_tile(slot)
    half = D_MODEL // 2
    w = xbuf[slot]
    x_lo = pltpu.bitcast(w << 16, f32).astype(bf16)
    x_hi = pltpu.bitcast(w & jnp.uint32(0xFFFF0000), f32).astype(bf16)

    def proj(w_ref):
        return (jnp.dot(x_lo, w_ref[0, :half, :], preferred_element_type=f32)
                + jnp.dot(x_hi, w_ref[0, half:, :], preferred_element_type=f32))

    hg = proj(wg_ref)
    for i in range(tc // 2):
        x_copy(idxn_ref, i, 1 - slot).start(priority=i % 2)
    hu = proj(wu_ref)
    hid = (hg * _sigmoid(hg) * hu).astype(bf16)
    for i in range(tc // 2, tc):
        x_copy(idxn_ref, i, 1 - slot).start(priority=i % 2)
    ye_ref[...] = jnp.dot(hid, wd_ref[0], preferred_element_type=f32).astype(bf16)

    @pl.when(s == last)
    def _():
        wait_tile(1 - slot)


def _ffn(idx, h2, wg, wu, wd):
    e, cap = idx.shape
    tc = _tile(cap, 256)
    nt = cap // tc
    steps = e * nt
    wspec = pl.BlockSpec((1, D_MODEL, D_FF), lambda si: (si // nt, 0, 0))
    idx3 = idx.reshape(steps, 1, tc)
    return pl.pallas_call(
        _ffn_kernel,
        grid=(steps,),
        in_specs=[
            pl.BlockSpec((1, 1, tc), lambda si: (si, 0, 0), memory_space=pltpu.SMEM),
            pl.BlockSpec((1, 1, tc), lambda si: (jnp.minimum(si + 1, steps - 1), 0, 0), memory_space=pltpu.SMEM),
            pl.BlockSpec(memory_space=pl.ANY),
            wspec, wspec,
            pl.BlockSpec((1, D_FF, D_MODEL), lambda si: (si // nt, 0, 0)),
        ],
        out_specs=pl.BlockSpec((tc, D_MODEL), lambda si: (si, 0)),
        out_shape=jax.ShapeDtypeStruct((e * cap, D_MODEL), bf16),
        scratch_shapes=[pltpu.VMEM((2, tc, D_MODEL // 2), jnp.uint32), pltpu.SemaphoreType.DMA((2,))],
        compiler_params=_cparams(("arbitrary",)),
        name="expert_ffn",
    )(idx3, idx3, h2, wg, wu, wd)


COMB_CH = 32
COMB_KB = 256
COMB_ALIGN = 2 * SUBLANES_V7X
META_START, META_NCH, META_OFF, META_SLOT, META_TOT = 0, 16, 32, 48, 64
META_W = 128


def _combine_kernel(meta_ref, metan_ref, x1_ref, pos_ref, aff_ref, ye_hbm, g_ref, o_ref, slab, oht, acc, sem):
    tn = x1_ref.shape[0]
    i = pl.program_id(0)
    last = pl.num_programs(0) - 1
    buf = i % 2

    def pieces(mref, b, fn):
        for e in range(N_EXPERTS):
            def body(c, carry, e=e):
                src = ye_hbm.at[pl.ds(pl.multiple_of(mref[0, 0, META_START + e] + c * COMB_CH, COMB_ALIGN),
                                      COMB_CH)]
                dst = slab.at[b, pl.ds(pl.multiple_of(mref[0, 0, META_OFF + e] + c * COMB_CH, COMB_CH), COMB_CH)]
                fn(pltpu.make_async_copy(src, dst, sem.at[b]))
                return carry
            lax.fori_loop(0, mref[0, 0, META_NCH + e], body, 0)

    @pl.when(i == 0)
    def _():
        slab[...] = jnp.zeros_like(slab)
        pieces(meta_ref, 0, lambda cp: cp.start())

    @pl.when(i < last)
    def _():
        pieces(metan_ref, 1 - buf, lambda cp: cp.start())

    rowi = lax.broadcasted_iota(i32, (COMB_CH, 1), 0)
    for e in range(N_EXPERTS):
        pos_e = pos_ref[e:e + 1, :]
        gate_e = aff_ref[e:e + 1, :]

        def fill(c, carry, e=e, pos_e=pos_e, gate_e=gate_e):
            slot = meta_ref[0, 0, META_SLOT + e] + c * COMB_CH + rowi
            row0 = pl.multiple_of(meta_ref[0, 0, META_OFF + e] + c * COMB_CH, COMB_CH)
            oht[pl.ds(row0, COMB_CH), :] = jnp.where(pos_e == slot, gate_e, 0.0).astype(bf16)
            return carry
        lax.fori_loop(0, meta_ref[0, 0, META_NCH + e], fill, 0)

    ktot = meta_ref[0, 0, META_TOT]
    nkb = (ktot + COMB_KB - 1) // COMB_KB

    def clear(c, carry):
        row0 = pl.multiple_of(ktot + c * COMB_CH, COMB_CH)
        oht[pl.ds(row0, COMB_CH), :] = jnp.zeros((COMB_CH, tn), bf16)
        return carry
    lax.fori_loop(0, (nkb * COMB_KB - ktot) // COMB_CH, clear, 0)

    pieces(meta_ref, buf, lambda cp: cp.wait())
    acc[...] = x1_ref[...]

    def mm(j, carry):
        rows = pl.ds(pl.multiple_of(j * COMB_KB, COMB_KB), COMB_KB)
        acc[...] += lax.dot_general(oht[rows, :], slab[buf, rows, :], TN, preferred_element_type=f32)
        return carry
    lax.fori_loop(0, nkb, mm, 0)
    o_ref[...] = _rms(acc[...], g_ref[...])


def _combine(meta, x1, pos, aff_t, ye, g):
    n = x1.shape[0]
    tn = _tile(n, 256)
    nt = n // tn
    kmax = N_EXPERTS * (tn + 2 * COMB_CH)
    assert kmax % COMB_KB == 0
    return pl.pallas_call(
        _combine_kernel,
        grid=(nt,),
        in_specs=[
            pl.BlockSpec((1, 1, META_W), lambda i: (i, 0, 0), memory_space=pltpu.SMEM),
            pl.BlockSpec((1, 1, META_W), lambda i: (jnp.minimum(i + 1, nt - 1), 0, 0), memory_space=pltpu.SMEM),
            pl.BlockSpec((tn, D_MODEL), lambda i: (i, 0)),
            pl.BlockSpec((N_EXPERTS, tn), lambda i: (0, i)),
            pl.BlockSpec((N_EXPERTS, tn), lambda i: (0, i)),
            pl.BlockSpec(memory_space=pl.ANY),
            pl.BlockSpec((1, D_MODEL), lambda i: (0, 0)),
        ],
        out_specs=pl.BlockSpec((tn, D_MODEL), lambda i: (i, 0)),
        out_shape=jax.ShapeDtypeStruct((n, D_MODEL), f32),
        scratch_shapes=[pltpu.VMEM((2, kmax, D_MODEL), bf16), pltpu.VMEM((kmax, tn), bf16),
                        pltpu.VMEM((tn, D_MODEL), f32), pltpu.SemaphoreType.DMA((2,))],
        compiler_params=_cparams(("arbitrary",)),
        name="combine_norm",
    )(meta, meta, x1, pos, aff_t, ye, g)


def _combine_meta(cum, cap, tn):
    e, n = cum.shape
    nt = n // tn
    end = cum[:, tn - 1::tn]
    before = jnp.concatenate([jnp.zeros((e, 1), i32), end[:, :-1]], axis=1)
    base = jnp.arange(e, dtype=i32)[:, None] * cap
    start = (base + before) // COMB_ALIGN * COMB_ALIGN
    nch = jnp.where(end > before, (base + end - start + COMB_CH - 1) // COMB_CH, 0)
    start = jnp.minimum(start, e * cap - nch * COMB_CH)
    rows = nch * COMB_CH
    off = jnp.cumsum(rows, axis=0) - rows
    tot = jnp.sum(rows, axis=0, keepdims=True)
    pad = jnp.zeros((META_W - META_TOT - 1, nt), i32)
    meta = jnp.concatenate([start, nch, off, start - base, tot, pad], axis=0)
    return meta.T.reshape(nt, 1, META_W)


def _prepare_params(norm_mix, w_in, shift_mu_prev, shift_mu_next, decay_w0, decay_up, iclr_a0, iclr_up,
                    gate_up, k_k, k_a, r_k, lnx_w, lnx_b, w_o_rwkv, dw_kernel, dw_bias, conv_ln_w, conv_ln_b,
                    w_o_conv, w_out, norm_ffn, router_w, router_b, exp_w_gate, exp_w_up, exp_w_down, norm_final):
    l = 0
    rwkv_cols = RKV_COLS + LORA_COLS
    w = w_in[l]
    pad = jnp.zeros((D_MODEL, LORA_PAD - LORA_COLS), f32)
    wa = jnp.concatenate([w[:, :rwkv_cols], pad], axis=1).astype(bf16)
    wu = w[:, rwkv_cols:rwkv_cols + 2 * C_CONV].astype(bf16)
    wgate = w[:, rwkv_cols + 2 * C_CONV:].astype(bf16)
    padv = jnp.zeros((LORA_PAD - LORA_COLS,), f32)
    mup = jnp.concatenate([shift_mu_prev[l], padv])[None, :]
    mun = jnp.concatenate([shift_mu_next[l], padv])[None, :]
    wup = jnp.zeros((LORA_PAD, UP_COLS), f32)
    wup = wup.at[0:W_LORA, 0:D_RWKV].set(decay_up[l, 0])
    wup = wup.at[W_LORA:2 * W_LORA, D_RWKV:2 * D_RWKV].set(decay_up[l, 1])
    o = 2 * W_LORA
    wup = wup.at[o:o + A_LORA, 2 * D_RWKV:3 * D_RWKV].set(iclr_up[l, 0])
    wup = wup.at[o + A_LORA:o + 2 * A_LORA, 3 * D_RWKV:4 * D_RWKV].set(iclr_up[l, 1])
    o = 2 * W_LORA + 2 * A_LORA
    wup = wup.at[o:o + G_LORA, 4 * D_RWKV:5 * D_RWKV].set(gate_up[l])
    head = jnp.arange(D_RWKV) // HEAD
    same = (head[:, None] == head[None, :])
    return dict(
        norm_mix=norm_mix[l][None, :], wa=wa, wu=wu, wgate=wgate, mup=mup, mun=mun, wup=wup.astype(bf16),
        w0=decay_w0[l], a0=iclr_a0[l], kkw=k_k[l][None, :], ka=k_a[l][None, :], rk=r_k[l].reshape(1, D_RWKV),
        hsum=same.astype(bf16), havg=(same.astype(f32) / HEAD).astype(bf16),
        lnxw=lnx_w[l][None, :], lnxb=lnx_b[l][None, :], wor=w_o_rwkv[l].astype(bf16),
        kern=dw_kernel[l], cbias=dw_bias[l][None, :], clnw=conv_ln_w[l][None, :], clnb=conv_ln_b[l][None, :],
        woc=w_o_conv[l].astype(bf16), wout=w_out[l].astype(bf16), nffn=norm_ffn[l][None, :],
        rwt=router_w[l].T.astype(bf16), rb=router_b[l][:, None],
        ewg=exp_w_gate[l].astype(bf16), ewu=exp_w_up[l].astype(bf16), ewd=exp_w_down[l].astype(bf16),
        nfinal=norm_final[None, :],
    )


def _trunk(x, P):
    b, t, d = x.shape
    n = b * t
    x2 = x.reshape(n, d)
    p, u = _inproj(x2, P["norm_mix"], P["wa"], P["wu"])
    r, v, kk, ew0, ew1, k0, k1, b0, b1, g, bonus = _prep(
        p.reshape(b, t, PA_COLS), P["mup"], P["mun"], P["wup"], P["w0"], P["a0"], P["kkw"], P["ka"], P["rk"],
        P["hsum"])
    yf, yb = _scan(r, v, kk, ew0, ew1, k0, k1, b0, b1)
    flat = lambda a: a.reshape(n, a.shape[-1])
    x1, h2, aff_t = _merge(t, x2, flat(yf), flat(yb), flat(g), flat(bonus), u, P["kern"], P["cbias"],
                           P["clnw"], P["clnb"], P["norm_mix"], P["wgate"], P["lnxw"], P["lnxb"], P["havg"],
                           P["wor"], P["woc"], P["wout"], P["nffn"], P["rwt"], P["rb"])
    cap = max(1, CAPACITY_FACTOR * n // N_EXPERTS)
    idx, cum, pos = _select(aff_t, cap)
    ye = _ffn(idx, h2, P["ewg"], P["ewu"], P["ewd"])
    meta = _combine_meta(cum, cap, _tile(n, 256))
    return _combine(meta, x1, pos, aff_t, ye, P["nfinal"]).reshape(b, t, d)


def kernel(x_prompt, x_sample, norm_mix, w_in, shift_mu_prev, shift_mu_next, decay_w0, decay_up, iclr_a0, iclr_up, gate_up, k_k, k_a, r_k, lnx_w, lnx_b, w_o_rwkv, dw_kernel, dw_bias, conv_ln_w, conv_ln_b, w_o_conv, w_out, norm_ffn, router_w, router_b, exp_w_gate, exp_w_up, exp_w_down, norm_final):
    P = _prepare_params(norm_mix, w_in, shift_mu_prev, shift_mu_next, decay_w0, decay_up, iclr_a0, iclr_up,
                        gate_up, k_k, k_a, r_k, lnx_w, lnx_b, w_o_rwkv, dw_kernel, dw_bias, conv_ln_w,
                        conv_ln_b, w_o_conv, w_out, norm_ffn, router_w, router_b, exp_w_gate, exp_w_up,
                        exp_w_down, norm_final)
    return (_trunk(x_prompt, P), _trunk(x_sample, P))
```

```python
import functools
import math

import jax
import jax.numpy as jnp
from jax import lax
from jax.experimental import pallas as pl
from jax.experimental.pallas import tpu as pltpu

f32 = jnp.float32
bf16 = jnp.bfloat16
i32 = jnp.int32

D_MODEL = 1024
D_RWKV = 512
HEAD = 64
W_LORA = 32
A_LORA = 32
G_LORA = 96
C_CONV = 512
CONV_W = 31
N_EXPERTS = 16
D_FF = 1024
CAPACITY_FACTOR = 2
GN_EPS = 64e-5
LN_EPS = 1e-5
RMS_EPS = 1e-6
LORA_COLS = 2 * W_LORA + 2 * A_LORA + G_LORA
LORA_PAD = 256
RKV_COLS = 3 * D_RWKV
PA_COLS = RKV_COLS + LORA_PAD
UP_COLS = 5 * D_RWKV

LANES_V7X = 128
SUBLANES_V7X = 8
MXU_DIM_V7X = 256
VMEM_LIMIT_V7X = 56 * 1024 * 1024

CHUNK = 64
SCAN_UNROLL = 4
SCAN_BLOCK = 512
GROUP = MXU_DIM_V7X // HEAD
GROUP_LANES = GROUP * HEAD
N_GROUPS = D_RWKV // GROUP_LANES

NN = (((1,), (0,)), ((), ()))
NT = (((1,), (1,)), ((), ()))
TN = (((0,), (0,)), ((), ()))


def _mm(a, b, dims=NN):
    return lax.dot_general(a.astype(bf16), b.astype(bf16), dims, preferred_element_type=f32)


def _mm_split(a, x, n):
    out = None
    rem = x
    for _ in range(n):
        hi = rem.astype(bf16)
        t = lax.dot_general(a, hi, NN, preferred_element_type=f32)
        out = t if out is None else out + t
        rem = rem - hi.astype(f32)
    return out


def _mm_split_lhs(x, a, n):
    out = None
    rem = x
    for _ in range(n):
        hi = rem.astype(bf16)
        t = lax.dot_general(hi, a, NN, preferred_element_type=f32)
        out = t if out is None else out + t
        rem = rem - hi.astype(f32)
    return out


def _sigmoid(x):
    return 1.0 / (1.0 + jnp.exp(-x))


def _rms(x, g):
    return x * lax.rsqrt(jnp.mean(x * x, axis=-1, keepdims=True) + RMS_EPS) * g


def _cparams(sem):
    return pltpu.CompilerParams(dimension_semantics=sem, vmem_limit_bytes=VMEM_LIMIT_V7X)


def _tile(n, pref):
    t = min(n, pref)
    assert n % t == 0, (n, pref)
    return t


def _inproj_kernel(x_ref, g_ref, wa_ref, wu_ref, p_ref, u_ref):
    h = _rms(x_ref[...], g_ref[...]).astype(bf16)
    p_ref[...] = jnp.dot(h, wa_ref[...], preferred_element_type=f32).astype(bf16)
    u_ref[...] = jnp.dot(h, wu_ref[...], preferred_element_type=f32).astype(bf16)


def _inproj(x2, g, wa, wu):
    n = x2.shape[0]
    tm = _tile(n, 512)
    return pl.pallas_call(
        _inproj_kernel,
        grid=(n // tm,),
        in_specs=[
            pl.BlockSpec((tm, D_MODEL), lambda i: (i, 0)),
            pl.BlockSpec((1, D_MODEL), lambda i: (0, 0)),
            pl.BlockSpec((D_MODEL, PA_COLS), lambda i: (0, 0)),
            pl.BlockSpec((D_MODEL, 2 * C_CONV), lambda i: (0, 0)),
        ],
        out_specs=[
            pl.BlockSpec((tm, PA_COLS), lambda i: (i, 0)),
            pl.BlockSpec((tm, 2 * C_CONV), lambda i: (i, 0)),
        ],
        out_shape=[
            jax.ShapeDtypeStruct((n, PA_COLS), bf16),
            jax.ShapeDtypeStruct((n, 2 * C_CONV), bf16),
        ],
        compiler_params=_cparams(("parallel",)),
        name="inproj",
    )(x2, g, wa, wu)


PREP_HALO = 16


def _prep_kernel(pc_ref, pp_ref, pn_ref, mup_ref, mun_ref, wup_ref, w0_ref, a0_ref, kkw_ref, ka_ref,
                 rk_ref, hsum_ref, r_o, v_o, kk_o, ew0_o, ew1_o, k0_o, k1_o, b0_o, b1_o, g_o, bonus_o):
    tt = pc_ref.shape[1]
    i = pl.program_id(1)
    nt = pl.num_programs(1)
    row = lax.broadcasted_iota(i32, (tt, 1), 0)
    first = i == 0
    last = i == nt - 1

    def shifted(c0, c1):
        p = pc_ref[0, :, c0:c1].astype(f32)
        prev_row = jnp.where(first, 0.0, pp_ref[0, PREP_HALO - 1:PREP_HALO, c0:c1].astype(f32))
        next_row = jnp.where(last, 0.0, pn_ref[0, 0:1, c0:c1].astype(f32))
        zp = jnp.where(row == 0, prev_row, pltpu.roll(p, 1, axis=0))
        zn = jnp.where(row == tt - 1, next_row, pltpu.roll(p, tt - 1, axis=0))
        return p + mup_ref[:, c0:c1] * (zp - p) + mun_ref[:, c0:c1] * (zn - p)

    r = shifted(0, D_RWKV)
    k = shifted(D_RWKV, 2 * D_RWKV)
    v = shifted(2 * D_RWKV, 3 * D_RWKV)
    zl = shifted(RKV_COLS, PA_COLS)

    lane = lax.broadcasted_iota(i32, (1, LORA_PAD), 1)
    act = jnp.where(lane < 2 * W_LORA, jnp.tanh(zl),
                    jnp.where(lane < 2 * W_LORA + 2 * A_LORA, zl, _sigmoid(zl)))
    pre = jnp.dot(act.astype(bf16), wup_ref[...], preferred_element_type=f32)

    hsum = hsum_ref[...]
    kkr = k * kkw_ref[...]
    ss = _mm_split_lhs(kkr * kkr, hsum, 1)
    kk = kkr / jnp.maximum(jnp.sqrt(ss), 1e-12)
    ka = ka_ref[...]
    ksum = None
    outs = ((ew0_o, k0_o, b0_o), (ew1_o, k1_o, b1_o))
    for d in range(2):
        x = pre[:, d * D_RWKV:(d + 1) * D_RWKV] + w0_ref[d:d + 1, :]
        ew = math.exp(-0.5) * _sigmoid(x)
        alpha = _sigmoid(pre[:, (2 + d) * D_RWKV:(3 + d) * D_RWKV] + a0_ref[d:d + 1, :])
        kd = k * (1.0 + (alpha - 1.0) * ka)
        ew_o, k_o, b_o = outs[d]
        ew_o[0] = ew
        k_o[0] = kd.astype(bf16)
        b_o[0] = (kk * alpha).astype(bf16)
        ksum = kd if ksum is None else ksum + kd
    bonus = _mm_split_lhs(r * ksum * rk_ref[...], hsum, 1) * v
    r_o[0] = r.astype(bf16)
    v_o[0] = v.astype(bf16)
    kk_o[0] = kk.astype(bf16)
    g_o[0] = pre[:, 4 * D_RWKV:5 * D_RWKV].astype(bf16)
    bonus_o[0] = bonus.astype(bf16)


def _prep(p3, mup, mun, wup, w0, a0, kkw, ka, rk, hsum):
    b, t, _ = p3.shape
    tt = _tile(t, 256)
    nt = t // tt
    hb = tt // PREP_HALO
    full = lambda s: pl.BlockSpec(s, lambda bi, ti: tuple(0 for _ in s))
    o_spec = pl.BlockSpec((1, tt, D_RWKV), lambda bi, ti: (bi, ti, 0))
    o_lo = jax.ShapeDtypeStruct((b, t, D_RWKV), bf16)
    o_hi = jax.ShapeDtypeStruct((b, t, D_RWKV), f32)
    return pl.pallas_call(
        _prep_kernel,
        grid=(b, nt),
        in_specs=[
            pl.BlockSpec((1, tt, PA_COLS), lambda bi, ti: (bi, ti, 0)),
            pl.BlockSpec((1, PREP_HALO, PA_COLS), lambda bi, ti: (bi, jnp.maximum(ti * hb - 1, 0), 0)),
            pl.BlockSpec((1, PREP_HALO, PA_COLS),
                         lambda bi, ti: (bi, jnp.minimum((ti + 1) * hb, t // PREP_HALO - 1), 0)),
            full((1, PA_COLS)), full((1, PA_COLS)), full((LORA_PAD, UP_COLS)),
            full((2, D_RWKV)), full((2, D_RWKV)), full((1, D_RWKV)), full((1, D_RWKV)), full((1, D_RWKV)),
            full((D_RWKV, D_RWKV)),
        ],
        out_specs=[o_spec] * 11,
        out_shape=[o_lo, o_lo, o_lo, o_hi, o_hi] + [o_lo] * 6,
        compiler_params=_cparams(("parallel", "parallel")),
        name="rwkv_prep",
    )(p3, p3, p3, mup, mun, wup, w0, a0, kkw, ka, rk, hsum)


def _scan_consts(reverse):
    L = CHUNK
    t = lax.broadcasted_iota(i32, (L, GROUP * L), 0)
    s = lax.broadcasted_iota(i32, (L, GROUP * L), 1) & (L - 1)
    strict = (s > t) if reverse else (s < t)
    eye = s == t
    incl = strict | eye
    tt = lax.broadcasted_iota(i32, (L, L), 0)
    ss = lax.broadcasted_iota(i32, (L, L), 1)
    tri = jnp.where((ss >= tt) if reverse else (ss <= tt), 1.0, 0.0).astype(bf16)
    rr = lax.broadcasted_iota(i32, (GROUP_LANES, GROUP_LANES), 0) // HEAD
    cc = lax.broadcasted_iota(i32, (GROUP_LANES, GROUP_LANES), 1) // HEAD
    same = rr == cc
    return strict, incl, eye, tri, (same, jnp.where(same, 1.0, 0.0).astype(bf16))


def _bd(x, bdmask):
    xb = x.astype(bf16)
    return jnp.concatenate([xb] * GROUP, axis=0) * bdmask[1]


def _wkv_chunks(chains):
    L = CHUNK
    n = len(chains)
    rng = range(n)
    r, v, kk, ew, k, b, consts, rev = (list(c) for c in zip(*chains))
    strict = [c[0] for c in consts]
    incl = [c[1] for c in consts]
    eye = [c[2] for c in consts]
    tri = [c[3] for c in consts]
    bdm = consts[0][4]
    ginc = [-_mm_split(tri[i], ew[i], 2) for i in rng]
    gref = [0.5 * (ginc[i][0:1] if rev[i] else ginc[i][L - 1:L]) for i in rng]
    egref = [jnp.exp(gref[i]) for i in rng]
    e1 = [jnp.exp(ginc[i] - gref[i]) for i in rng]
    e2 = [jnp.exp(gref[i] - ginc[i]) for i in rng]
    ea = [jnp.exp(ginc[i] + ew[i] - gref[i]) for i in rng]
    At = [-kk[i] * ea[i] for i in rng]
    Rt = [r[i] * e1[i] for i in rng]
    Kt = [k[i] * e2[i] for i in rng]
    Bt = [b[i] * e2[i] for i in rng]
    Be = [Bt[i] * egref[i] for i in rng]
    AR = [jnp.concatenate([At[i], Rt[i]], axis=0) for i in rng]
    pb = [_mm(AR[i], _bd(Bt[i], bdm), NT) for i in rng]
    pk = [_mm(AR[i], _bd(Kt[i], bdm), NT) for i in rng]
    Mab = [jnp.where(strict[i], pb[i][:L], 0.0) for i in rng]
    Arb = [jnp.where(incl[i], pb[i][L:], 0.0) for i in rng]
    Mak = [jnp.where(strict[i], pk[i][:L], 0.0) for i in rng]
    Ark = [jnp.where(incl[i], pk[i][L:], 0.0) for i in rng]
    vbd = [_bd(v[i], bdm) for i in rng]
    MakV = [_mm(Mak[i], vbd[i]) for i in rng]
    X = [jnp.where(eye[i], 1.0, Mab[i]) for i in rng]
    P = [_mm(Mab[i], _bd(Mab[i], bdm)) for i in rng]
    for step in range(4):
        XP = [_mm(jnp.concatenate([X[i], P[i]], axis=0), _bd(P[i], bdm)) for i in rng]
        X = [X[i] + XP[i][:L] for i in rng]
        P = [XP[i][L:] for i in rng]
    X = [X[i] + _mm(X[i], _bd(P[i], bdm)) for i in rng]
    W1 = [_mm(X[i], _bd(At[i] * egref[i], bdm)) for i in rng]
    W2 = [_mm(X[i], _bd(MakV[i], bdm)) for i in rng]
    Q = [Rt[i] * egref[i] + _mm(Arb[i], _bd(W1[i], bdm)) for i in rng]
    Y0 = [_mm(Arb[i], _bd(W2[i], bdm)) + _mm(Ark[i], vbd[i]) for i in rng]
    Pm = [jnp.where(bdm[0], _mm(W1[i], Be[i], TN), 0.0) for i in rng]
    Qm = [jnp.where(bdm[0], _mm(jnp.concatenate([W2[i], v[i]], axis=0),
                                jnp.concatenate([Be[i], Kt[i] * egref[i]], axis=0), TN), 0.0) for i in rng]
    D = [egref[i] * egref[i] for i in rng]
    return Q, Y0, Pm, Qm, D


def _scan_kernel(rf, vf, kkf, ewf, kf, bf, rb, vb, kkb, ewb, kb, bb, yf_ref, yb_ref, s_ref):
    tb = rf.shape[1]
    nch = tb // CHUNK

    @pl.when(pl.program_id(1) == 0)
    def _():
        s_ref[...] = jnp.zeros_like(s_ref)

    consts_f = _scan_consts(False)
    consts_b = _scan_consts(True)

    def body(j, carry):
        items, dests = [], []
        for sub in range(SCAN_UNROLL):
            for reverse, refs, y_ref, consts in ((False, (rf, vf, kkf, ewf, kf, bf), yf_ref, consts_f),
                                                 (True, (rb, vb, kkb, ewb, kb, bb), yb_ref, consts_b)):
                step = j * SCAN_UNROLL + sub
                cj = (nch - 1 - step) if reverse else step
                rows = pl.ds(pl.multiple_of(cj * CHUNK, CHUNK), CHUNK)
                for g in range(N_GROUPS):
                    lanes = slice(g * GROUP_LANES, (g + 1) * GROUP_LANES)
                    items.append(tuple(ref[0, rows, lanes].astype(f32) for ref in refs) + (consts, reverse))
                    dests.append((y_ref, rows, lanes, (2 if reverse else 0) + g))
        Q, Y0, Pm, Qm, D = _wkv_chunks(items)
        S = [s_ref[si] for si in range(2 * N_GROUPS)]
        for i, (y_ref, rows, lanes, si) in enumerate(dests):
            y_ref[0, rows, lanes] = _mm(Q[i], S[si], NT) + Y0[i]
            S[si] = S[si] * D[i] + _mm(S[si], Pm[i]) + Qm[i]
        for si in range(2 * N_GROUPS):
            s_ref[si] = S[si]
        return carry

    lax.fori_loop(0, nch // SCAN_UNROLL, body, 0)


def _scan(r, v, kk, ew0, ew1, k0, k1, b0, b1):
    b, t, _ = r.shape
    tb = _tile(t, SCAN_BLOCK)
    assert tb % (CHUNK * SCAN_UNROLL) == 0
    nb = t // tb
    fwd = pl.BlockSpec((1, tb, D_RWKV), lambda bi, ti: (bi, ti, 0))
    bwd = pl.BlockSpec((1, tb, D_RWKV), lambda bi, ti: (bi, nb - 1 - ti, 0))
    o_shape = jax.ShapeDtypeStruct((b, t, D_RWKV), f32)
    return pl.pallas_call(
        _scan_kernel,
        grid=(b, nb),
        in_specs=[fwd] * 6 + [bwd] * 6,
        out_specs=[fwd, bwd],
        out_shape=[o_shape, o_shape],
        scratch_shapes=[pltpu.VMEM((2 * N_GROUPS, GROUP_LANES, GROUP_LANES), f32)],
        compiler_params=_cparams(("parallel", "arbitrary")),
        name="wkv_scan",
    )(r, v, kk, ew0, k0, b0, r, v, kk, ew1, k1, b1)


CONV_HALO = 16
CONV_ROWS = 32
MERGE_SPLIT = 2


def _conv_tile(uc_ref, up_ref, un_ref, kern_ref, bias_ref, lnw_ref, lnb_ref, ext_ref, sh_ref, first, last):
    tt = uc_ref.shape[0]

    def glu(u):
        u = u.astype(f32)
        return u[:, :C_CONV] * _sigmoid(u[:, C_CONV:])

    ext_ref[0:CONV_HALO, :] = jnp.where(first, 0.0, glu(up_ref[...]))
    ext_ref[CONV_HALO:CONV_HALO + tt, :] = glu(uc_ref[...])
    ext_ref[CONV_HALO + tt:, :] = jnp.where(last, 0.0, glu(un_ref[...]))
    span = sh_ref.shape[1]
    for r in range(1, SUBLANES_V7X):
        sh_ref[r] = ext_ref[r:r + span, :]
    half = CONV_W // 2
    kern = [kern_ref[j:j + 1, :] for j in range(CONV_W)]
    outs = []
    for r0 in range(0, tt, CONV_ROWS):
        acc = jnp.zeros((CONV_ROWS, C_CONV), f32) + bias_ref[...]
        for j in range(CONV_W):
            off = CONV_HALO - half + j
            r, base = off % SUBLANES_V7X, off - off % SUBLANES_V7X + r0
            tap = ext_ref[base:base + CONV_ROWS, :] if r == 0 else sh_ref[r, base:base + CONV_ROWS, :]
            acc = acc + kern[j] * tap
        mu = jnp.mean(acc, axis=-1, keepdims=True)
        xc = acc - mu
        var = jnp.mean(xc * xc, axis=-1, keepdims=True)
        y = xc * lax.rsqrt(var + LN_EPS) * lnw_ref[...] + lnb_ref[...]
        outs.append(y * _sigmoid(y))
    return jnp.concatenate(outs, axis=0)


def _merge_kernel(x_ref, yf_ref, yb_ref, g_ref, bonus_ref, uc_ref, up_ref, un_ref, kern_ref, cbias_ref,
                  clnw_ref, clnb_ref, nmix_ref, wg_ref, lnxw_ref, lnxb_ref,
                  havg_ref, wor_ref, woc_ref, wout_ref, nffn_ref, rwt_ref, rb_ref,
                  x1_ref, h2_ref, aff_ref, ext_ref, sh_ref, *, tiles_per_seq):
    ti = pl.program_id(0) % tiles_per_seq
    tm = x_ref.shape[0]
    cb = _conv_tile(uc_ref, up_ref, un_ref, kern_ref, cbias_ref, clnw_ref, clnb_ref, ext_ref, sh_ref,
                    ti == 0, ti == tiles_per_seq - 1)
    rs = [slice(i * tm // MERGE_SPLIT, (i + 1) * tm // MERGE_SPLIT) for i in range(MERGE_SPLIT)]
    havg = havg_ref[...]
    half = D_MODEL // 2
    x = [x_ref[r, :] for r in rs]
    h = [_rms(xi, nmix_ref[...]).astype(bf16) for xi in x]
    wkv = [yf_ref[r, :] + yb_ref[r, :] for r in rs]
    mu = [_mm_split_lhs(w, havg, 1) for w in wkv]
    gates = [_sigmoid(jnp.dot(hi, wg_ref[...], preferred_element_type=f32)) for hi in h]
    xc = [w - m for w, m in zip(wkv, mu)]
    var = [_mm_split_lhs(c * c, havg, 1) for c in xc]
    y = [(c * lax.rsqrt(v + GN_EPS) * lnxw_ref[...] + lnxb_ref[...] + bonus_ref[r, :]) * g_ref[r, :]
         for c, v, r in zip(xc, var, rs)]
    y_a = [jnp.dot(yi.astype(bf16), wor_ref[...], preferred_element_type=f32) for yi in y]
    y_b = [jnp.dot(cb[r, :].astype(bf16), woc_ref[...], preferred_element_type=f32) for r in rs]
    m = [gi[:, :D_MODEL] * a + gi[:, D_MODEL:] * b for gi, a, b in zip(gates, y_a, y_b)]
    x1 = [xi + jnp.dot(mi.astype(bf16), wout_ref[...], preferred_element_type=f32) for xi, mi in zip(x, m)]
    h2 = [_rms(xi, nffn_ref[...]).astype(bf16) for xi in x1]
    for r, x1i, h2i in zip(rs, x1, h2):
        x1_ref[r, :] = x1i
        hbits = pltpu.bitcast(h2i.astype(f32), jnp.uint32)
        h2_ref[r, :] = (hbits[:, :half] >> 16) | (hbits[:, half:] & jnp.uint32(0xFFFF0000))
        logits = lax.dot_general(rwt_ref[...], h2i, NT, preferred_element_type=f32) + rb_ref[...]
        mx = jnp.max(logits, axis=0, keepdims=True)
        e = jnp.exp(logits - mx)
        aff_ref[:, r] = e / jnp.sum(e, axis=0, keepdims=True)


def _merge(t, x2, yf, yb, g, bonus, u2, kern, cbias, clnw, clnb, nmix, wg, lnxw, lnxb, havg, wor, woc, wout,
           nffn, rwt, rb):
    n = x2.shape[0]
    tm = _tile(t, 512)
    hb = tm // CONV_HALO
    row = lambda c: pl.BlockSpec((tm, c), lambda i: (i, 0))
    full = lambda s: pl.BlockSpec(s, lambda i: tuple(0 for _ in s))
    return pl.pallas_call(
        functools.partial(_merge_kernel, tiles_per_seq=t // tm),
        grid=(n // tm,),
        in_specs=[
            row(D_MODEL), row(D_RWKV), row(D_RWKV), row(D_RWKV), row(D_RWKV), row(2 * C_CONV),
            pl.BlockSpec((CONV_HALO, 2 * C_CONV), lambda i: (jnp.maximum(i * hb - 1, 0), 0)),
            pl.BlockSpec((CONV_HALO, 2 * C_CONV), lambda i: (jnp.minimum((i + 1) * hb, n // CONV_HALO - 1), 0)),
            full((CONV_W, C_CONV)), full((1, C_CONV)), full((1, C_CONV)), full((1, C_CONV)),
            full((1, D_MODEL)), full((D_MODEL, 2 * D_MODEL)), full((1, D_RWKV)), full((1, D_RWKV)),
            full((D_RWKV, D_RWKV)), full((D_RWKV, D_MODEL)), full((C_CONV, D_MODEL)),
            full((D_MODEL, D_MODEL)), full((1, D_MODEL)), full((N_EXPERTS, D_MODEL)), full((N_EXPERTS, 1)),
        ],
        out_specs=[row(D_MODEL), row(D_MODEL // 2), pl.BlockSpec((N_EXPERTS, tm), lambda i: (0, i))],
        out_shape=[
            jax.ShapeDtypeStruct((n, D_MODEL), f32),
            jax.ShapeDtypeStruct((n, D_MODEL // 2), jnp.uint32),
            jax.ShapeDtypeStruct((N_EXPERTS, n), f32),
        ],
        scratch_shapes=[pltpu.VMEM((tm + 2 * CONV_HALO, C_CONV), f32),
                        pltpu.VMEM((SUBLANES_V7X, tm + 2 * CONV_HALO - SUBLANES_V7X, C_CONV), f32)],
        compiler_params=_cparams(("parallel",)),
        name="merge_router",
    )(x2, yf, yb, g, bonus, u2, u2, u2, kern, cbias, clnw, clnb, nmix, wg, lnxw, lnxb, havg, wor, woc, wout,
      nffn, rwt, rb)


def _thr_kernel(aff_ref, thr_ref, need_ref, *, cap):
    def count_ge(mid):
        bits = pltpu.bitcast(aff_ref[...], i32)
        return jnp.sum(jnp.where(bits >= mid, 1.0, 0.0), axis=1, keepdims=True)

    def body(_, c):
        lo, hi = c
        mid = lo + ((hi - lo + 1) >> 1)
        ok = count_ge(mid) >= cap
        return jnp.where(ok, mid, lo), jnp.where(ok, hi, mid - 1)

    lo0 = jnp.zeros((N_EXPERTS, 1), i32)
    hi0 = jnp.full((N_EXPERTS, 1), 0x3F800000, i32)
    lo, _ = lax.fori_loop(0, 31, body, (lo0, hi0))
    thr_ref[...] = jnp.broadcast_to(lo, thr_ref.shape)
    bits = pltpu.bitcast(aff_ref[...], i32)
    need = cap - jnp.sum(jnp.where(bits > lo, 1.0, 0.0), axis=1, keepdims=True)
    need_ref[...] = jnp.broadcast_to(need, need_ref.shape)


def _threshold(aff_t, cap):
    e, n = aff_t.shape
    o_spec = pl.BlockSpec((e, LANES_V7X), lambda: (0, 0))
    return pl.pallas_call(
        functools.partial(_thr_kernel, cap=float(cap)),
        in_specs=[pl.BlockSpec((e, n), lambda: (0, 0))],
        out_specs=[o_spec, o_spec],
        out_shape=[jax.ShapeDtypeStruct((e, LANES_V7X), i32), jax.ShapeDtypeStruct((e, LANES_V7X), f32)],
        compiler_params=pltpu.CompilerParams(vmem_limit_bytes=VMEM_LIMIT_V7X),
        name="route_threshold",
    )(aff_t)


RANK_BLOCK = 2048
NOT_ROUTED = -(1 << 30)


def _rank_kernel(aff_ref, thr_ref, need_ref, tri_ref, cum_ref, pos_ref, carry):
    @pl.when(pl.program_id(0) == 0)
    def _():
        carry[...] = jnp.zeros_like(carry)

    thr = thr_ref[:, :1]
    need = need_ref[:, :1]
    tri = tri_ref[...]
    c1 = carry[0][:, :1]
    c2 = carry[1][:, :1]
    for g in range(aff_ref.shape[1] // LANES_V7X):
        lanes = slice(g * LANES_V7X, (g + 1) * LANES_V7X)
        bits = pltpu.bitcast(aff_ref[:, lanes], i32)
        eq = bits == thr
        r1 = jnp.dot(jnp.where(eq, 1.0, 0.0).astype(bf16), tri, preferred_element_type=f32) + c1
        sel = (bits > thr) | (eq & (r1 <= need))
        r2 = jnp.dot(jnp.where(sel, 1.0, 0.0).astype(bf16), tri, preferred_element_type=f32) + c2
        r2i = r2.astype(i32)
        cum_ref[:, lanes] = r2i
        pos_ref[:, lanes] = jnp.where(sel, r2i - 1, NOT_ROUTED)
        c1 = r1[:, LANES_V7X - 1:]
        c2 = r2[:, LANES_V7X - 1:]
    carry[0] = jnp.broadcast_to(c1, carry.shape[1:])
    carry[1] = jnp.broadcast_to(c2, carry.shape[1:])


def _rank(aff_t, thr, need):
    e, n = aff_t.shape
    tb = _tile(n, RANK_BLOCK)
    r = jnp.arange(LANES_V7X)
    tri = (r[:, None] <= r[None, :]).astype(bf16)
    small = pl.BlockSpec((e, LANES_V7X), lambda i: (0, 0))
    blk = pl.BlockSpec((e, tb), lambda i: (0, i))
    return pl.pallas_call(
        _rank_kernel,
        grid=(n // tb,),
        in_specs=[blk, small, small, pl.BlockSpec((LANES_V7X, LANES_V7X), lambda i: (0, 0))],
        out_specs=[blk, blk],
        out_shape=[jax.ShapeDtypeStruct((e, n), i32), jax.ShapeDtypeStruct((e, n), i32)],
        scratch_shapes=[pltpu.VMEM((2, e, LANES_V7X), f32)],
        compiler_params=_cparams(("arbitrary",)),
        name="route_rank",
    )(aff_t, thr, need, tri)


def _compact_kernel(bc_ref, cum_ref, idx_ref):
    e = pl.program_id(0)
    cap = idx_ref.shape[2]
    L = LANES_V7X
    pcol0 = lax.broadcasted_iota(i32, (L, 1), 0)
    eye = lax.broadcasted_iota(i32, (L, L), 0) == lax.broadcasted_iota(i32, (L, L), 1)

    def tile(j, b_lo):
        p0 = j * L
        b_lo = lax.while_loop(lambda b: bc_ref[e, b] <= p0, lambda b: b + 1, b_lo)
        pcol = p0 + pcol0

        def add_group(b, acc):
            row = cum_ref[0, :, pl.ds(pl.multiple_of(b * L, L), L)]
            return acc + jnp.where(row <= pcol, 1.0, 0.0)

        def cond(c):
            return bc_ref[e, c[0] - 1] < p0 + L

        def body(c):
            return c[0] + 1, add_group(c[0], c[1])

        _, acc = lax.while_loop(cond, body, (b_lo + 1, add_group(b_lo, jnp.zeros((L, L), f32))))
        col = jnp.sum(acc, axis=1, keepdims=True) + (b_lo * L).astype(f32)
        row = jnp.sum(jnp.where(eye, col, 0.0), axis=0, keepdims=True)
        idx_ref[0, :, pl.ds(pl.multiple_of(p0, L), L)] = row.astype(i32)
        return b_lo

    lax.fori_loop(0, cap // L, tile, 0)


def _compact(cum, cap):
    e, n = cum.shape
    bc = cum[:, LANES_V7X - 1::LANES_V7X]
    idx = pl.pallas_call(
        _compact_kernel,
        grid=(e,),
        in_specs=[pl.BlockSpec(memory_space=pltpu.SMEM), pl.BlockSpec((1, 1, n), lambda ei: (ei, 0, 0))],
        out_specs=pl.BlockSpec((1, 1, cap), lambda ei: (ei, 0, 0)),
        out_shape=jax.ShapeDtypeStruct((e, 1, cap), i32),
        compiler_params=_cparams(("arbitrary",)),
        name="route_compact",
    )(bc, cum.reshape(e, 1, n))
    return idx.reshape(e, cap)


def _select(aff_t, cap):
    thr, need = _threshold(aff_t, cap)
    cum, pos = _rank(aff_t, thr, need)
    return _compact(cum, cap), cum, pos


def _ffn_kernel(idx_ref, idxn_ref, h2_hbm, wg_ref, wu_ref, wd_ref, ye_ref, xbuf, sem):
    tc = xbuf.shape[1]
    s = pl.program_id(0)
    last = pl.num_programs(0) - 1
    slot = s % 2

    def x_copy(ref, i, sl):
        return pltpu.make_async_copy(h2_hbm.at[pl.ds(ref[0, 0, i], 1)], xbuf.at[sl, pl.ds(i, 1)], sem.at[sl])

    def wait_tile(sl):
        pltpu.make_async_copy(h2_hbm.at[pl.ds(0, tc)], xbuf.at[sl], sem.at[sl]).wait()

    @pl.when(s == 0)
    def _():
        def body(i, c):
            x_copy(idx_ref, i, 0).start()
            return c
        lax.fori_loop(0, tc, body, 0, unroll=8)

    wait_tile(slot)
    half = D_MODEL // 2
    w = xbuf[slot]
    x_lo = pltpu.bitcast(w << 16, f32).astype(bf16)
    x_hi = pltpu.bitcast(w & jnp.uint32(0xFFFF0000), f32).astype(bf16)

    def proj(w_ref):
        return (jnp.dot(x_lo, w_ref[0, :half, :], preferred_element_type=f32)
                + jnp.dot(x_hi, w_ref[0, half:, :], preferred_element_type=f32))

    hg = proj(wg_ref)
    for i in range(tc // 2):
        x_copy(idxn_ref, i, 1 - slot).start(priority=i % 2)
    hu = proj(wu_ref)
    hid = (hg * _sigmoid(hg) * hu).astype(bf16)
    for i in range(tc // 2, tc):
        x_copy(idxn_ref, i, 1 - slot).start(priority=i % 2)
    ye_ref[...] = jnp.dot(hid, wd_ref[0], preferred_element_type=f32).astype(bf16)

    @pl.when(s == last)
    def _():
        wait_tile(1 - slot)


def _ffn(idx, h2, wg, wu, wd):
    e, cap = idx.shape
    tc = _tile(cap, 256)
    nt = cap // tc
    steps = e * nt
    wspec = pl.BlockSpec((1, D_MODEL, D_FF), lambda si: (si // nt, 0, 0))
    idx3 = idx.reshape(steps, 1, tc)
    return pl.pallas_call(
        _ffn_kernel,
        grid=(steps,),
        in_specs=[
            pl.BlockSpec((1, 1, tc), lambda si: (si, 0, 0), memory_space=pltpu.SMEM),
            pl.BlockSpec((1, 1, tc), lambda si: (jnp.minimum(si + 1, steps - 1), 0, 0), memory_space=pltpu.SMEM),
            pl.BlockSpec(memory_space=pl.ANY),
            wspec, wspec,
            pl.BlockSpec((1, D_FF, D_MODEL), lambda si: (si // nt, 0, 0)),
        ],
        out_specs=pl.BlockSpec((tc, D_MODEL), lambda si: (si, 0)),
        out_shape=jax.ShapeDtypeStruct((e * cap, D_MODEL), bf16),
        scratch_shapes=[pltpu.VMEM((2, tc, D_MODEL // 2), jnp.uint32), pltpu.SemaphoreType.DMA((2,))],
        compiler_params=_cparams(("arbitrary",)),
        name="expert_ffn",
    )(idx3, idx3, h2, wg, wu, wd)


COMB_CH = 32
COMB_KB = 256
COMB_ALIGN = 2 * SUBLANES_V7X
META_START, META_NCH, META_OFF, META_SLOT, META_TOT = 0, 16, 32, 48, 64
META_W = 128


def _combine_kernel(meta_ref, metan_ref, x1_ref, pos_ref, aff_ref, ye_hbm, g_ref, o_ref, slab, oht, acc, sem):
    tn = x1_ref.shape[0]
    i = pl.program_id(0)
    last = pl.num_programs(0) - 1
    buf = i % 2

    def pieces(mref, b, fn):
        for e in range(N_EXPERTS):
            def body(c, carry, e=e):
                src = ye_hbm.at[pl.ds(pl.multiple_of(mref[0, 0, META_START + e] + c * COMB_CH, COMB_ALIGN),
                                      COMB_CH)]
                dst = slab.at[b, pl.ds(pl.multiple_of(mref[0, 0, META_OFF + e] + c * COMB_CH, COMB_CH), COMB_CH)]
                fn(pltpu.make_async_copy(src, dst, sem.at[b]))
                return carry
            lax.fori_loop(0, mref[0, 0, META_NCH + e], body, 0)

    @pl.when(i == 0)
    def _():
        slab[...] = jnp.zeros_like(slab)
        pieces(meta_ref, 0, lambda cp: cp.start())

    @pl.when(i < last)
    def _():
        pieces(metan_ref, 1 - buf, lambda cp: cp.start())

    rowi = lax.broadcasted_iota(i32, (COMB_CH, 1), 0)
    for e in range(N_EXPERTS):
        pos_e = pos_ref[e:e + 1, :]
        gate_e = aff_ref[e:e + 1, :]

        def fill(c, carry, e=e, pos_e=pos_e, gate_e=gate_e):
            slot = meta_ref[0, 0, META_SLOT + e] + c * COMB_CH + rowi
            row0 = pl.multiple_of(meta_ref[0, 0, META_OFF + e] + c * COMB_CH, COMB_CH)
            oht[pl.ds(row0, COMB_CH), :] = jnp.where(pos_e == slot, gate_e, 0.0).astype(bf16)
            return carry
        lax.fori_loop(0, meta_ref[0, 0, META_NCH + e], fill, 0)

    ktot = meta_ref[0, 0, META_TOT]
    nkb = (ktot + COMB_KB - 1) // COMB_KB

    def clear(c, carry):
        row0 = pl.multiple_of(ktot + c * COMB_CH, COMB_CH)
        oht[pl.ds(row0, COMB_CH), :] = jnp.zeros((COMB_CH, tn), bf16)
        return carry
    lax.fori_loop(0, (nkb * COMB_KB - ktot) // COMB_CH, clear, 0)

    pieces(meta_ref, buf, lambda cp: cp.wait())
    acc[...] = x1_ref[...]

    def mm(j, carry):
        rows = pl.ds(pl.multiple_of(j * COMB_KB, COMB_KB), COMB_KB)
        acc[...] += lax.dot_general(oht[rows, :], slab[buf, rows, :], TN, preferred_element_type=f32)
        return carry
    lax.fori_loop(0, nkb, mm, 0)
    o_ref[...] = _rms(acc[...], g_ref[...])


def _combine(meta, x1, pos, aff_t, ye, g):
    n = x1.shape[0]
    tn = _tile(n, 256)
    nt = n // tn
    kmax = N_EXPERTS * (tn + 2 * COMB_CH)
    assert kmax % COMB_KB == 0
    return pl.pallas_call(
        _combine_kernel,
        grid=(nt,),
        in_specs=[
            pl.BlockSpec((1, 1, META_W), lambda i: (i, 0, 0), memory_space=pltpu.SMEM),
            pl.BlockSpec((1, 1, META_W), lambda i: (jnp.minimum(i + 1, nt - 1), 0, 0), memory_space=pltpu.SMEM),
            pl.BlockSpec((tn, D_MODEL), lambda i: (i, 0)),
            pl.BlockSpec((N_EXPERTS, tn), lambda i: (0, i)),
            pl.BlockSpec((N_EXPERTS, tn), lambda i: (0, i)),
            pl.BlockSpec(memory_space=pl.ANY),
            pl.BlockSpec((1, D_MODEL), lambda i: (0, 0)),
        ],
        out_specs=pl.BlockSpec((tn, D_MODEL), lambda i: (i, 0)),
        out_shape=jax.ShapeDtypeStruct((n, D_MODEL), f32),
        scratch_shapes=[pltpu.VMEM((2, kmax, D_MODEL), bf16), pltpu.VMEM((kmax, tn), bf16),
                        pltpu.VMEM((tn, D_MODEL), f32), pltpu.SemaphoreType.DMA((2,))],
        compiler_params=_cparams(("arbitrary",)),
        name="combine_norm",
    )(meta, meta, x1, pos, aff_t, ye, g)


def _combine_meta(cum, cap, tn):
    e, n = cum.shape
    nt = n // tn
    end = cum[:, tn - 1::tn]
    before = jnp.concatenate([jnp.zeros((e, 1), i32), end[:, :-1]], axis=1)
    base = jnp.arange(e, dtype=i32)[:, None] * cap
    start = (base + before) // COMB_ALIGN * COMB_ALIGN
    nch = jnp.where(end > before, (base + end - start + COMB_CH - 1) // COMB_CH, 0)
    start = jnp.minimum(start, e * cap - nch * COMB_CH)
    rows = nch * COMB_CH
    off = jnp.cumsum(rows, axis=0) - rows
    tot = jnp.sum(rows, axis=0, keepdims=True)
    pad = jnp.zeros((META_W - META_TOT - 1, nt), i32)
    meta = jnp.concatenate([start, nch, off, start - base, tot, pad], axis=0)
    return meta.T.reshape(nt, 1, META_W)


def _prepare_params(norm_mix, w_in, shift_mu_prev, shift_mu_next, decay_w0, decay_up, iclr_a0, iclr_up,
                    gate_up, k_k, k_a, r_k, lnx_w, lnx_b, w_o_rwkv, dw_kernel, dw_bias, conv_ln_w, conv_ln_b,
                    w_o_conv, w_out, norm_ffn, router_w, router_b, exp_w_gate, exp_w_up, exp_w_down, norm_final):
    l = 0
    rwkv_cols = RKV_COLS + LORA_COLS
    w = w_in[l]
    pad = jnp.zeros((D_MODEL, LORA_PAD - LORA_COLS), f32)
    wa = jnp.concatenate([w[:, :rwkv_cols], pad], axis=1).astype(bf16)
    wu = w[:, rwkv_cols:rwkv_cols + 2 * C_CONV].astype(bf16)
    wgate = w[:, rwkv_cols + 2 * C_CONV:].astype(bf16)
    padv = jnp.zeros((LORA_PAD - LORA_COLS,), f32)
    mup = jnp.concatenate([shift_mu_prev[l], padv])[None, :]
    mun = jnp.concatenate([shift_mu_next[l], padv])[None, :]
    wup = jnp.zeros((LORA_PAD, UP_COLS), f32)
    wup = wup.at[0:W_LORA, 0:D_RWKV].set(decay_up[l, 0])
    wup = wup.at[W_LORA:2 * W_LORA, D_RWKV:2 * D_RWKV].set(decay_up[l, 1])
    o = 2 * W_LORA
    wup = wup.at[o:o + A_LORA, 2 * D_RWKV:3 * D_RWKV].set(iclr_up[l, 0])
    wup = wup.at[o + A_LORA:o + 2 * A_LORA, 3 * D_RWKV:4 * D_RWKV].set(iclr_up[l, 1])
    o = 2 * W_LORA + 2 * A_LORA
    wup = wup.at[o:o + G_LORA, 4 * D_RWKV:5 * D_RWKV].set(gate_up[l])
    head = jnp.arange(D_RWKV) // HEAD
    same = (head[:, None] == head[None, :])
    return dict(
        norm_mix=norm_mix[l][None, :], wa=wa, wu=wu, wgate=wgate, mup=mup, mun=mun, wup=wup.astype(bf16),
        w0=decay_w0[l], a0=iclr_a0[l], kkw=k_k[l][None, :], ka=k_a[l][None, :], rk=r_k[l].reshape(1, D_RWKV),
        hsum=same.astype(bf16), havg=(same.astype(f32) / HEAD).astype(bf16),
        lnxw=lnx_w[l][None, :], lnxb=lnx_b[l][None, :], wor=w_o_rwkv[l].astype(bf16),
        kern=dw_kernel[l], cbias=dw_bias[l][None, :], clnw=conv_ln_w[l][None, :], clnb=conv_ln_b[l][None, :],
        woc=w_o_conv[l].astype(bf16), wout=w_out[l].astype(bf16), nffn=norm_ffn[l][None, :],
        rwt=router_w[l].T.astype(bf16), rb=router_b[l][:, None],
        ewg=exp_w_gate[l].astype(bf16), ewu=exp_w_up[l].astype(bf16), ewd=exp_w_down[l].astype(bf16),
        nfinal=norm_final[None, :],
    )


def _trunk(x, P):
    b, t, d = x.shape
    n = b * t
    x2 = x.reshape(n, d)
    p, u = _inproj(x2, P["norm_mix"], P["wa"], P["wu"])
    r, v, kk, ew0, ew1, k0, k1, b0, b1, g, bonus = _prep(
        p.reshape(b, t, PA_COLS), P["mup"], P["mun"], P["wup"], P["w0"], P["a0"], P["kkw"], P["ka"], P["rk"],
        P["hsum"])
    yf, yb = _scan(r, v, kk, ew0, ew1, k0, k1, b0, b1)
    flat = lambda a: a.reshape(n, a.shape[-1])
    x1, h2, aff_t = _merge(t, x2, flat(yf), flat(yb), flat(g), flat(bonus), u, P["kern"], P["cbias"],
                           P["clnw"], P["clnb"], P["norm_mix"], P["wgate"], P["lnxw"], P["lnxb"], P["havg"],
                           P["wor"], P["woc"], P["wout"], P["nffn"], P["rwt"], P["rb"])
    cap = max(1, CAPACITY_FACTOR * n // N_EXPERTS)
    idx, cum, pos = _select(aff_t, cap)
    ye = _ffn(idx, h2, P["ewg"], P["ewu"], P["ewd"])
    meta = _combine_meta(cum, cap, _tile(n, 256))
    return _combine(meta, x1, pos, aff_t, ye, P["nfinal"]).reshape(b, t, d)


def kernel(x_prompt, x_sample, norm_mix, w_in, shift_mu_prev, shift_mu_next, decay_w0, decay_up, iclr_a0, iclr_up, gate_up, k_k, k_a, r_k, lnx_w, lnx_b, w_o_rwkv, dw_kernel, dw_bias, conv_ln_w, conv_ln_b, w_o_conv, w_out, norm_ffn, router_w, router_b, exp_w_gate, exp_w_up, exp_w_down, norm_final):
    P = _prepare_params(norm_mix, w_in, shift_mu_prev, shift_mu_next, decay_w0, decay_up, iclr_a0, iclr_up,
                        gate_up, k_k, k_a, r_k, lnx_w, lnx_b, w_o_rwkv, dw_kernel, dw_bias, conv_ln_w,
                        conv_ln_b, w_o_conv, w_out, norm_ffn, router_w, router_b, exp_w_gate, exp_w_up,
                        exp_w_down, norm_final)
    return (_trunk(x_prompt, P), _trunk(x_sample, P))
```

```python
import functools
import math

import jax
import jax.numpy as jnp
from jax import lax
from jax.experimental import pallas as pl
from jax.experimental.pallas import tpu as pltpu

f32 = jnp.float32
bf16 = jnp.bfloat16
i32 = jnp.int32

D_MODEL = 1024
D_RWKV = 512
HEAD = 64
W_LORA = 32
A_LORA = 32
G_LORA = 96
C_CONV = 512
CONV_W = 31
N_EXPERTS = 16
D_FF = 1024
CAPACITY_FACTOR = 2
GN_EPS = 64e-5
LN_EPS = 1e-5
RMS_EPS = 1e-6
LORA_COLS = 2 * W_LORA + 2 * A_LORA + G_LORA
LORA_PAD = 256
RKV_COLS = 3 * D_RWKV
PA_COLS = RKV_COLS + LORA_PAD
UP_COLS = 5 * D_RWKV

LANES_V7X = 128
SUBLANES_V7X = 8
MXU_DIM_V7X = 256
VMEM_LIMIT_V7X = 56 * 1024 * 1024

CHUNK = 64
SCAN_UNROLL = 4
SCAN_BLOCK = 512
GROUP = MXU_DIM_V7X // HEAD
GROUP_LANES = GROUP * HEAD
N_GROUPS = D_RWKV // GROUP_LANES

NN = (((1,), (0,)), ((), ()))
NT = (((1,), (1,)), ((), ()))
TN = (((0,), (0,)), ((), ()))


def _mm(a, b, dims=NN):
    return lax.dot_general(a.astype(bf16), b.astype(bf16), dims, preferred_element_type=f32)


def _mm_split(a, x, n):
    out = None
    rem = x
    for _ in range(n):
        hi = rem.astype(bf16)
        t = lax.dot_general(a, hi, NN, preferred_element_type=f32)
        out = t if out is None else out + t
        rem = rem - hi.astype(f32)
    return out


def _mm_split_lhs(x, a, n):
    out = None
    rem = x
    for _ in range(n):
        hi = rem.astype(bf16)
        t = lax.dot_general(hi, a, NN, preferred_element_type=f32)
        out = t if out is None else out + t
        rem = rem - hi.astype(f32)
    return out


def _sigmoid(x):
    return 1.0 / (1.0 + jnp.exp(-x))


def _rms(x, g):
    return x * lax.rsqrt(jnp.mean(x * x, axis=-1, keepdims=True) + RMS_EPS) * g


def _cparams(sem):
    return pltpu.CompilerParams(dimension_semantics=sem, vmem_limit_bytes=VMEM_LIMIT_V7X)


def _tile(n, pref):
    t = min(n, pref)
    assert n % t == 0, (n, pref)
    return t


def _inproj_kernel(x_ref, g_ref, wa_ref, wu_ref, p_ref, u_ref):
    h = _rms(x_ref[...], g_ref[...]).astype(bf16)
    p_ref[...] = jnp.dot(h, wa_ref[...], preferred_element_type=f32).astype(bf16)
    u_ref[...] = jnp.dot(h, wu_ref[...], preferred_element_type=f32).astype(bf16)


def _inproj(x2, g, wa, wu):
    n = x2.shape[0]
    tm = _tile(n, 512)
    return pl.pallas_call(
        _inproj_kernel,
        grid=(n // tm,),
        in_specs=[
            pl.BlockSpec((tm, D_MODEL), lambda i: (i, 0)),
            pl.BlockSpec((1, D_MODEL), lambda i: (0, 0)),
            pl.BlockSpec((D_MODEL, PA_COLS), lambda i: (0, 0)),
            pl.BlockSpec((D_MODEL, 2 * C_CONV), lambda i: (0, 0)),
        ],
        out_specs=[
            pl.BlockSpec((tm, PA_COLS), lambda i: (i, 0)),
            pl.BlockSpec((tm, 2 * C_CONV), lambda i: (i, 0)),
        ],
        out_shape=[
            jax.ShapeDtypeStruct((n, PA_COLS), bf16),
            jax.ShapeDtypeStruct((n, 2 * C_CONV), bf16),
        ],
        compiler_params=_cparams(("parallel",)),
        name="inproj",
    )(x2, g, wa, wu)


PREP_HALO = 16


def _prep_kernel(pc_ref, pp_ref, pn_ref, mup_ref, mun_ref, wup_ref, w0_ref, a0_ref, kkw_ref, ka_ref,
                 rk_ref, hsum_ref, r_o, v_o, kk_o, ew0_o, ew1_o, k0_o, k1_o, b0_o, b1_o, g_o, bonus_o):
    tt = pc_ref.shape[1]
    i = pl.program_id(1)
    nt = pl.num_programs(1)
    row = lax.broadcasted_iota(i32, (tt, 1), 0)
    first = i == 0
    last = i == nt - 1

    def shifted(c0, c1):
        p = pc_ref[0, :, c0:c1].astype(f32)
        prev_row = jnp.where(first, 0.0, pp_ref[0, PREP_HALO - 1:PREP_HALO, c0:c1].astype(f32))
        next_row = jnp.where(last, 0.0, pn_ref[0, 0:1, c0:c1].astype(f32))
        zp = jnp.where(row == 0, prev_row, pltpu.roll(p, 1, axis=0))
        zn = jnp.where(row == tt - 1, next_row, pltpu.roll(p, tt - 1, axis=0))
        return p + mup_ref[:, c0:c1] * (zp - p) + mun_ref[:, c0:c1] * (zn - p)

    r = shifted(0, D_RWKV)
    k = shifted(D_RWKV, 2 * D_RWKV)
    v = shifted(2 * D_RWKV, 3 * D_RWKV)
    zl = shifted(RKV_COLS, PA_COLS)

    lane = lax.broadcasted_iota(i32, (1, LORA_PAD), 1)
    act = jnp.where(lane < 2 * W_LORA, jnp.tanh(zl),
                    jnp.where(lane < 2 * W_LORA + 2 * A_LORA, zl, _sigmoid(zl)))
    pre = jnp.dot(act.astype(bf16), wup_ref[...], preferred_element_type=f32)

    hsum = hsum_ref[...]
    kkr = k * kkw_ref[...]
    ss = _mm_split_lhs(kkr * kkr, hsum, 1)
    kk = kkr / jnp.maximum(jnp.sqrt(ss), 1e-12)
    ka = ka_ref[...]
    ksum = None
    outs = ((ew0_o, k0_o, b0_o), (ew1_o, k1_o, b1_o))
    for d in range(2):
        x = pre[:, d * D_RWKV:(d + 1) * D_RWKV] + w0_ref[d:d + 1, :]
        ew = math.exp(-0.5) * _sigmoid(x)
        alpha = _sigmoid(pre[:, (2 + d) * D_RWKV:(3 + d) * D_RWKV] + a0_ref[d:d + 1, :])
        kd = k * (1.0 + (alpha - 1.0) * ka)
        ew_o, k_o, b_o = outs[d]
        ew_o[0] = ew
        k_o[0] = kd.astype(bf16)
        b_o[0] = (kk * alpha).astype(bf16)
        ksum = kd if ksum is None else ksum + kd
    bonus = _mm_split_lhs(r * ksum * rk_ref[...], hsum, 1) * v
    r_o[0] = r.astype(bf16)
    v_o[0] = v.astype(bf16)
    kk_o[0] = kk.astype(bf16)
    g_o[0] = pre[:, 4 * D_RWKV:5 * D_RWKV].astype(bf16)
    bonus_o[0] = bonus.astype(bf16)


def _prep(p3, mup, mun, wup, w0, a0, kkw, ka, rk, hsum):
    b, t, _ = p3.shape
    tt = _tile(t, 256)
    nt = t // tt
    hb = tt // PREP_HALO
    full = lambda s: pl.BlockSpec(s, lambda bi, ti: tuple(0 for _ in s))
    o_spec = pl.BlockSpec((1, tt, D_RWKV), lambda bi, ti: (bi, ti, 0))
    o_lo = jax.ShapeDtypeStruct((b, t, D_RWKV), bf16)
    o_hi = jax.ShapeDtypeStruct((b, t, D_RWKV), f32)
    return pl.pallas_call(
        _prep_kernel,
        grid=(b, nt),
        in_specs=[
            pl.BlockSpec((1, tt, PA_COLS), lambda bi, ti: (bi, ti, 0)),
            pl.BlockSpec((1, PREP_HALO, PA_COLS), lambda bi, ti: (bi, jnp.maximum(ti * hb - 1, 0), 0)),
            pl.BlockSpec((1, PREP_HALO, PA_COLS),
                         lambda bi, ti: (bi, jnp.minimum((ti + 1) * hb, t // PREP_HALO - 1), 0)),
            full((1, PA_COLS)), full((1, PA_COLS)), full((LORA_PAD, UP_COLS)),
            full((2, D_RWKV)), full((2, D_RWKV)), full((1, D_RWKV)), full((1, D_RWKV)), full((1, D_RWKV)),
            full((D_RWKV, D_RWKV)),
        ],
        out_specs=[o_spec] * 11,
        out_shape=[o_lo, o_lo, o_lo, o_hi, o_hi] + [o_lo] * 6,
        compiler_params=_cparams(("parallel", "parallel")),
        name="rwkv_prep",
    )(p3, p3, p3, mup, mun, wup, w0, a0, kkw, ka, rk, hsum)


def _scan_consts(reverse):
    L = CHUNK
    t = lax.broadcasted_iota(i32, (L, GROUP * L), 0)
    s = lax.broadcasted_iota(i32, (L, GROUP * L), 1) & (L - 1)
    strict = (s > t) if reverse else (s < t)
    eye = s == t
    incl = strict | eye
    tt = lax.broadcasted_iota(i32, (L, L), 0)
    ss = lax.broadcasted_iota(i32, (L, L), 1)
    tri = jnp.where((ss >= tt) if reverse else (ss <= tt), 1.0, 0.0).astype(bf16)
    rr = lax.broadcasted_iota(i32, (GROUP_LANES, GROUP_LANES), 0) // HEAD
    cc = lax.broadcasted_iota(i32, (GROUP_LANES, GROUP_LANES), 1) // HEAD
    same = rr == cc
    near = ((t // 16 == s // 16), (t // 32 == s // 32))
    return strict, incl, eye, tri, (same, jnp.where(same, 1.0, 0.0).astype(bf16)), near


def _bd(x, bdmask):
    xb = x.astype(bf16)
    return jnp.concatenate([xb] * GROUP, axis=0) * bdmask[1]


def _wkv_chunks(chains):
    L = CHUNK
    n = len(chains)
    rng = range(n)
    r, v, kk, ew, k, b, consts, rev = (list(c) for c in zip(*chains))
    strict = [c[0] for c in consts]
    incl = [c[1] for c in consts]
    eye = [c[2] for c in consts]
    tri = [c[3] for c in consts]
    bdm = consts[0][4]
    ginc = [-_mm_split(tri[i], ew[i], 2) for i in rng]
    gref = [0.5 * (ginc[i][0:1] if rev[i] else ginc[i][L - 1:L]) for i in rng]
    egref = [jnp.exp(gref[i]) for i in rng]
    e1 = [jnp.exp(ginc[i] - gref[i]) for i in rng]
    e2 = [jnp.exp(gref[i] - ginc[i]) for i in rng]
    ea = [jnp.exp(ginc[i] + ew[i] - gref[i]) for i in rng]
    At = [-kk[i] * ea[i] for i in rng]
    Rt = [r[i] * e1[i] for i in rng]
    Kt = [k[i] * e2[i] for i in rng]
    Bt = [b[i] * e2[i] for i in rng]
    Be = [Bt[i] * egref[i] for i in rng]
    AR = [jnp.concatenate([At[i], Rt[i]], axis=0) for i in rng]
    pb = [_mm(AR[i], _bd(Bt[i], bdm), NT) for i in rng]
    pk = [_mm(AR[i], _bd(Kt[i], bdm), NT) for i in rng]
    Mab = [jnp.where(strict[i], pb[i][:L], 0.0) for i in rng]
    Arb = [jnp.where(incl[i], pb[i][L:], 0.0) for i in rng]
    Mak = [jnp.where(strict[i], pk[i][:L], 0.0) for i in rng]
    Ark = [jnp.where(incl[i], pk[i][L:], 0.0) for i in rng]
    vbd = [_bd(v[i], bdm) for i in rng]
    MakV = [_mm(Mak[i], vbd[i]) for i in rng]
    near = [c[5] for c in consts]
    M16 = [jnp.where(near[i][0], Mab[i], 0.0) for i in rng]
    X = [jnp.where(eye[i], 1.0, M16[i]) for i in rng]
    P = [_mm(M16[i], _bd(M16[i], bdm)) for i in rng]
    for step in range(2):
        XP = [_mm(jnp.concatenate([X[i], P[i]], axis=0), _bd(P[i], bdm)) for i in rng]
        X = [X[i] + XP[i][:L] for i in rng]
        P = [XP[i][L:] for i in rng]
    X = [X[i] + _mm(X[i], _bd(P[i], bdm)) for i in rng]
    for lvl in range(2):
        off = [jnp.where(near[i][1] & ~near[i][0], Mab[i], 0.0) if lvl == 0
               else jnp.where(near[i][1], 0.0, Mab[i]) for i in rng]
        XN = [_mm(X[i], _bd(off[i], bdm)) for i in rng]
        X = [X[i] + _mm(XN[i], _bd(X[i], bdm)) for i in rng]
    W1 = [_mm(X[i], _bd(At[i] * egref[i], bdm)) for i in rng]
    W2 = [_mm(X[i], _bd(MakV[i], bdm)) for i in rng]
    Q = [Rt[i] * egref[i] + _mm(Arb[i], _bd(W1[i], bdm)) for i in rng]
    Y0 = [_mm(Arb[i], _bd(W2[i], bdm)) + _mm(Ark[i], vbd[i]) for i in rng]
    Pm = [jnp.where(bdm[0], _mm(W1[i], Be[i], TN), 0.0) for i in rng]
    Qm = [jnp.where(bdm[0], _mm(jnp.concatenate([W2[i], v[i]], axis=0),
                                jnp.concatenate([Be[i], Kt[i] * egref[i]], axis=0), TN), 0.0) for i in rng]
    D = [egref[i] * egref[i] for i in rng]
    return Q, Y0, Pm, Qm, D


def _scan_kernel(rf, vf, kkf, ewf, kf, bf, rb, vb, kkb, ewb, kb, bb, yf_ref, yb_ref, s_ref):
    tb = rf.shape[1]
    nch = tb // CHUNK

    @pl.when(pl.program_id(1) == 0)
    def _():
        s_ref[...] = jnp.zeros_like(s_ref)

    consts_f = _scan_consts(False)
    consts_b = _scan_consts(True)

    def body(j, carry):
        items, dests = [], []
        for sub in range(SCAN_UNROLL):
            for reverse, refs, y_ref, consts in ((False, (rf, vf, kkf, ewf, kf, bf), yf_ref, consts_f),
                                                 (True, (rb, vb, kkb, ewb, kb, bb), yb_ref, consts_b)):
                step = j * SCAN_UNROLL + sub
                cj = (nch - 1 - step) if reverse else step
                rows = pl.ds(pl.multiple_of(cj * CHUNK, CHUNK), CHUNK)
                for g in range(N_GROUPS):
                    lanes = slice(g * GROUP_LANES, (g + 1) * GROUP_LANES)
                    items.append(tuple(ref[0, rows, lanes].astype(f32) for ref in refs) + (consts, reverse))
                    dests.append((y_ref, rows, lanes, (2 if reverse else 0) + g))
        Q, Y0, Pm, Qm, D = _wkv_chunks(items)
        S = [s_ref[si] for si in range(2 * N_GROUPS)]
        for i, (y_ref, rows, lanes, si) in enumerate(dests):
            y_ref[0, rows, lanes] = _mm(Q[i], S[si], NT) + Y0[i]
            S[si] = S[si] * D[i] + _mm(S[si], Pm[i]) + Qm[i]
        for si in range(2 * N_GROUPS):
            s_ref[si] = S[si]
        return carry

    lax.fori_loop(0, nch // SCAN_UNROLL, body, 0)


def _scan(r, v, kk, ew0, ew1, k0, k1, b0, b1):
    b, t, _ = r.shape
    tb = _tile(t, SCAN_BLOCK)
    assert tb % (CHUNK * SCAN_UNROLL) == 0
    nb = t // tb
    fwd = pl.BlockSpec((1, tb, D_RWKV), lambda bi, ti: (bi, ti, 0))
    bwd = pl.BlockSpec((1, tb, D_RWKV), lambda bi, ti: (bi, nb - 1 - ti, 0))
    o_shape = jax.ShapeDtypeStruct((b, t, D_RWKV), f32)
    return pl.pallas_call(
        _scan_kernel,
        grid=(b, nb),
        in_specs=[fwd] * 6 + [bwd] * 6,
        out_specs=[fwd, bwd],
        out_shape=[o_shape, o_shape],
        scratch_shapes=[pltpu.VMEM((2 * N_GROUPS, GROUP_LANES, GROUP_LANES), f32)],
        compiler_params=_cparams(("parallel", "arbitrary")),
        name="wkv_scan",
    )(r, v, kk, ew0, k0, b0, r, v, kk, ew1, k1, b1)


CONV_HALO = 16
CONV_ROWS = 32
MERGE_SPLIT = 2


def _conv_tile(uc_ref, up_ref, un_ref, kern_ref, bias_ref, lnw_ref, lnb_ref, ext_ref, sh_ref, first, last):
    tt = uc_ref.shape[0]

    def glu(u):
        u = u.astype(f32)
        return u[:, :C_CONV] * _sigmoid(u[:, C_CONV:])

    ext_ref[0:CONV_HALO, :] = jnp.where(first, 0.0, glu(up_ref[...]))
    ext_ref[CONV_HALO:CONV_HALO + tt, :] = glu(uc_ref[...])
    ext_ref[CONV_HALO + tt:, :] = jnp.where(last, 0.0, glu(un_ref[...]))
    span = sh_ref.shape[1]
    for r in range(1, SUBLANES_V7X):
        sh_ref[r] = ext_ref[r:r + span, :]
    half = CONV_W // 2
    kern = [kern_ref[j:j + 1, :] for j in range(CONV_W)]
    outs = []
    for r0 in range(0, tt, CONV_ROWS):
        acc = jnp.zeros((CONV_ROWS, C_CONV), f32) + bias_ref[...]
        for j in range(CONV_W):
            off = CONV_HALO - half + j
            r, base = off % SUBLANES_V7X, off - off % SUBLANES_V7X + r0
            tap = ext_ref[base:base + CONV_ROWS, :] if r == 0 else sh_ref[r, base:base + CONV_ROWS, :]
            acc = acc + kern[j] * tap
        mu = jnp.mean(acc, axis=-1, keepdims=True)
        xc = acc - mu
        var = jnp.mean(xc * xc, axis=-1, keepdims=True)
        y = xc * lax.rsqrt(var + LN_EPS) * lnw_ref[...] + lnb_ref[...]
        outs.append(y * _sigmoid(y))
    return jnp.concatenate(outs, axis=0)


def _merge_kernel(x_ref, yf_ref, yb_ref, g_ref, bonus_ref, uc_ref, up_ref, un_ref, kern_ref, cbias_ref,
                  clnw_ref, clnb_ref, nmix_ref, wg_ref, lnxw_ref, lnxb_ref,
                  havg_ref, wor_ref, woc_ref, wout_ref, nffn_ref, rwt_ref, rb_ref,
                  x1_ref, h2_ref, aff_ref, ext_ref, sh_ref, *, tiles_per_seq):
    ti = pl.program_id(0) % tiles_per_seq
    tm = x_ref.shape[0]
    cb = _conv_tile(uc_ref, up_ref, un_ref, kern_ref, cbias_ref, clnw_ref, clnb_ref, ext_ref, sh_ref,
                    ti == 0, ti == tiles_per_seq - 1)
    rs = [slice(i * tm // MERGE_SPLIT, (i + 1) * tm // MERGE_SPLIT) for i in range(MERGE_SPLIT)]
    havg = havg_ref[...]
    half = D_MODEL // 2
    x = [x_ref[r, :] for r in rs]
    h = [_rms(xi, nmix_ref[...]).astype(bf16) for xi in x]
    wkv = [yf_ref[r, :] + yb_ref[r, :] for r in rs]
    mu = [_mm_split_lhs(w, havg, 1) for w in wkv]
    gates = [_sigmoid(jnp.dot(hi, wg_ref[...], preferred_element_type=f32)) for hi in h]
    xc = [w - m for w, m in zip(wkv, mu)]
    var = [_mm_split_lhs(c * c, havg, 1) for c in xc]
    y = [(c * lax.rsqrt(v + GN_EPS) * lnxw_ref[...] + lnxb_ref[...] + bonus_ref[r, :]) * g_ref[r, :]
         for c, v, r in zip(xc, var, rs)]
    y_a = [jnp.dot(yi.astype(bf16), wor_ref[...], preferred_element_type=f32) for yi in y]
    y_b = [jnp.dot(cb[r, :].astype(bf16), woc_ref[...], preferred_element_type=f32) for r in rs]
    m = [gi[:, :D_MODEL] * a + gi[:, D_MODEL:] * b for gi, a, b in zip(gates, y_a, y_b)]
    x1 = [xi + jnp.dot(mi.astype(bf16), wout_ref[...], preferred_element_type=f32) for xi, mi in zip(x, m)]
    h2 = [_rms(xi, nffn_ref[...]).astype(bf16) for xi in x1]
    for r, x1i, h2i in zip(rs, x1, h2):
        x1_ref[r, :] = x1i
        hbits = pltpu.bitcast(h2i.astype(f32), jnp.uint32)
        h2_ref[r, :] = (hbits[:, :half] >> 16) | (hbits[:, half:] & jnp.uint32(0xFFFF0000))
        logits = lax.dot_general(rwt_ref[...], h2i, NT, preferred_element_type=f32) + rb_ref[...]
        mx = jnp.max(logits, axis=0, keepdims=True)
        e = jnp.exp(logits - mx)
        aff_ref[:, r] = e / jnp.sum(e, axis=0, keepdims=True)


def _merge(t, x2, yf, yb, g, bonus, u2, kern, cbias, clnw, clnb, nmix, wg, lnxw, lnxb, havg, wor, woc, wout,
           nffn, rwt, rb):
    n = x2.shape[0]
    tm = _tile(t, 512)
    hb = tm // CONV_HALO
    row = lambda c: pl.BlockSpec((tm, c), lambda i: (i, 0))
    full = lambda s: pl.BlockSpec(s, lambda i: tuple(0 for _ in s))
    return pl.pallas_call(
        functools.partial(_merge_kernel, tiles_per_seq=t // tm),
        grid=(n // tm,),
        in_specs=[
            row(D_MODEL), row(D_RWKV), row(D_RWKV), row(D_RWKV), row(D_RWKV), row(2 * C_CONV),
            pl.BlockSpec((CONV_HALO, 2 * C_CONV), lambda i: (jnp.maximum(i * hb - 1, 0), 0)),
            pl.BlockSpec((CONV_HALO, 2 * C_CONV), lambda i: (jnp.minimum((i + 1) * hb, n // CONV_HALO - 1), 0)),
            full((CONV_W, C_CONV)), full((1, C_CONV)), full((1, C_CONV)), full((1, C_CONV)),
            full((1, D_MODEL)), full((D_MODEL, 2 * D_MODEL)), full((1, D_RWKV)), full((1, D_RWKV)),
            full((D_RWKV, D_RWKV)), full((D_RWKV, D_MODEL)), full((C_CONV, D_MODEL)),
            full((D_MODEL, D_MODEL)), full((1, D_MODEL)), full((N_EXPERTS, D_MODEL)), full((N_EXPERTS, 1)),
        ],
        out_specs=[row(D_MODEL), row(D_MODEL // 2), pl.BlockSpec((N_EXPERTS, tm), lambda i: (0, i))],
        out_shape=[
            jax.ShapeDtypeStruct((n, D_MODEL), f32),
            jax.ShapeDtypeStruct((n, D_MODEL // 2), jnp.uint32),
            jax.ShapeDtypeStruct((N_EXPERTS, n), f32),
        ],
        scratch_shapes=[pltpu.VMEM((tm + 2 * CONV_HALO, C_CONV), f32),
                        pltpu.VMEM((SUBLANES_V7X, tm + 2 * CONV_HALO - SUBLANES_V7X, C_CONV), f32)],
        compiler_params=_cparams(("parallel",)),
        name="merge_router",
    )(x2, yf, yb, g, bonus, u2, u2, u2, kern, cbias, clnw, clnb, nmix, wg, lnxw, lnxb, havg, wor, woc, wout,
      nffn, rwt, rb)


def _thr_kernel(aff_ref, thr_ref, need_ref, *, cap):
    def count_ge(mid):
        bits = pltpu.bitcast(aff_ref[...], i32)
        return jnp.sum(jnp.where(bits >= mid, 1.0, 0.0), axis=1, keepdims=True)

    def body(_, c):
        lo, hi = c
        mid = lo + ((hi - lo + 1) >> 1)
        ok = count_ge(mid) >= cap
        return jnp.where(ok, mid, lo), jnp.where(ok, hi, mid - 1)

    lo0 = jnp.zeros((N_EXPERTS, 1), i32)
    hi0 = jnp.full((N_EXPERTS, 1), 0x3F800000, i32)
    lo, _ = lax.fori_loop(0, 31, body, (lo0, hi0))
    thr_ref[...] = jnp.broadcast_to(lo, thr_ref.shape)
    bits = pltpu.bitcast(aff_ref[...], i32)
    need = cap - jnp.sum(jnp.where(bits > lo, 1.0, 0.0), axis=1, keepdims=True)
    need_ref[...] = jnp.broadcast_to(need, need_ref.shape)


def _threshold(aff_t, cap):
    e, n = aff_t.shape
    o_spec = pl.BlockSpec((e, LANES_V7X), lambda: (0, 0))
    return pl.pallas_call(
        functools.partial(_thr_kernel, cap=float(cap)),
        in_specs=[pl.BlockSpec((e, n), lambda: (0, 0))],
        out_specs=[o_spec, o_spec],
        out_shape=[jax.ShapeDtypeStruct((e, LANES_V7X), i32), jax.ShapeDtypeStruct((e, LANES_V7X), f32)],
        compiler_params=pltpu.CompilerParams(vmem_limit_bytes=VMEM_LIMIT_V7X),
        name="route_threshold",
    )(aff_t)


RANK_BLOCK = 2048
NOT_ROUTED = -(1 << 30)


def _rank_kernel(aff_ref, thr_ref, need_ref, tri_ref, cum_ref, pos_ref, carry):
    @pl.when(pl.program_id(0) == 0)
    def _():
        carry[...] = jnp.zeros_like(carry)

    thr = thr_ref[:, :1]
    need = need_ref[:, :1]
    tri = tri_ref[...]
    c1 = carry[0][:, :1]
    c2 = carry[1][:, :1]
    for g in range(aff_ref.shape[1] // LANES_V7X):
        lanes = slice(g * LANES_V7X, (g + 1) * LANES_V7X)
        bits = pltpu.bitcast(aff_ref[:, lanes], i32)
        eq = bits == thr
        r1 = jnp.dot(jnp.where(eq, 1.0, 0.0).astype(bf16), tri, preferred_element_type=f32) + c1
        sel = (bits > thr) | (eq & (r1 <= need))
        r2 = jnp.dot(jnp.where(sel, 1.0, 0.0).astype(bf16), tri, preferred_element_type=f32) + c2
        r2i = r2.astype(i32)
        cum_ref[:, lanes] = r2i
        pos_ref[:, lanes] = jnp.where(sel, r2i - 1, NOT_ROUTED)
        c1 = r1[:, LANES_V7X - 1:]
        c2 = r2[:, LANES_V7X - 1:]
    carry[0] = jnp.broadcast_to(c1, carry.shape[1:])
    carry[1] = jnp.broadcast_to(c2, carry.shape[1:])


def _rank(aff_t, thr, need):
    e, n = aff_t.shape
    tb = _tile(n, RANK_BLOCK)
    r = jnp.arange(LANES_V7X)
    tri = (r[:, None] <= r[None, :]).astype(bf16)
    small = pl.BlockSpec((e, LANES_V7X), lambda i: (0, 0))
    blk = pl.BlockSpec((e, tb), lambda i: (0, i))
    return pl.pallas_call(
        _rank_kernel,
        grid=(n // tb,),
        in_specs=[blk, small, small, pl.BlockSpec((LANES_V7X, LANES_V7X), lambda i: (0, 0))],
        out_specs=[blk, blk],
        out_shape=[jax.ShapeDtypeStruct((e, n), i32), jax.ShapeDtypeStruct((e, n), i32)],
        scratch_shapes=[pltpu.VMEM((2, e, LANES_V7X), f32)],
        compiler_params=_cparams(("arbitrary",)),
        name="route_rank",
    )(aff_t, thr, need, tri)


def _compact_kernel(bc_ref, cum_ref, idx_ref):
    e = pl.program_id(0)
    cap = idx_ref.shape[2]
    L = LANES_V7X
    pcol0 = lax.broadcasted_iota(i32, (L, 1), 0)
    eye = lax.broadcasted_iota(i32, (L, L), 0) == lax.broadcasted_iota(i32, (L, L), 1)

    def tile(j, b_lo):
        p0 = j * L
        b_lo = lax.while_loop(lambda b: bc_ref[e, b] <= p0, lambda b: b + 1, b_lo)
        pcol = p0 + pcol0

        def add_group(b, acc):
            row = cum_ref[0, :, pl.ds(pl.multiple_of(b * L, L), L)]
            return acc + jnp.where(row <= pcol, 1.0, 0.0)

        def cond(c):
            return bc_ref[e, c[0] - 1] < p0 + L

        def body(c):
            return c[0] + 1, add_group(c[0], c[1])

        _, acc = lax.while_loop(cond, body, (b_lo + 1, add_group(b_lo, jnp.zeros((L, L), f32))))
        col = jnp.sum(acc, axis=1, keepdims=True) + (b_lo * L).astype(f32)
        row = jnp.sum(jnp.where(eye, col, 0.0), axis=0, keepdims=True)
        idx_ref[0, :, pl.ds(pl.multiple_of(p0, L), L)] = row.astype(i32)
        return b_lo

    lax.fori_loop(0, cap // L, tile, 0)


def _compact(cum, cap):
    e, n = cum.shape
    bc = cum[:, LANES_V7X - 1::LANES_V7X]
    idx = pl.pallas_call(
        _compact_kernel,
        grid=(e,),
        in_specs=[pl.BlockSpec(memory_space=pltpu.SMEM), pl.BlockSpec((1, 1, n), lambda ei: (ei, 0, 0))],
        out_specs=pl.BlockSpec((1, 1, cap), lambda ei: (ei, 0, 0)),
        out_shape=jax.ShapeDtypeStruct((e, 1, cap), i32),
        compiler_params=_cparams(("arbitrary",)),
        name="route_compact",
    )(bc, cum.reshape(e, 1, n))
    return idx.reshape(e, cap)


def _select(aff_t, cap):
    thr, need = _threshold(aff_t, cap)
    cum, pos = _rank(aff_t, thr, need)
    return _compact(cum, cap), cum, pos


def _ffn_kernel(idx_ref, idxn_ref, h2_hbm, wg_ref, wu_ref, wd_ref, ye_ref, xbuf, sem):
    tc = xbuf.shape[1]
    s = pl.program_id(0)
    last = pl.num_programs(0) - 1
    slot = s % 2

    def x_copy(ref, i, sl):
        return pltpu.make_async_copy(h2_hbm.at[pl.ds(ref[0, 0, i], 1)], xbuf.at[sl, pl.ds(i, 1)], sem.at[sl])

    def wait_tile(sl):
        pltpu.make_async_copy(h2_hbm.at[pl.ds(0, tc)], xbuf.at[sl], sem.at[sl]).wait()

    @pl.when(s == 0)
    def _():
        def body(i, c):
            x_copy(idx_ref, i, 0).start()
            return c
        lax.fori_loop(0, tc, body, 0, unroll=8)

    wait_tile(slot)
    half = D_MODEL // 2
    w = xbuf[slot]
    x_lo = pltpu.bitcast(w << 16, f32).astype(bf16)
    x_hi = pltpu.bitcast(w & jnp.uint32(0xFFFF0000), f32).astype(bf16)

    def proj(w_ref):
        return (jnp.dot(x_lo, w_ref[0, :half, :], preferred_element_type=f32)
                + jnp.dot(x_hi, w_ref[0, half:, :], preferred_element_type=f32))

    hg = proj(wg_ref)
    for i in range(tc // 2):
        x_copy(idxn_ref, i, 1 - slot).start(priority=i % 2)
    hu = proj(wu_ref)
    hid = (hg * _sigmoid(hg) * hu).astype(bf16)
    for i in range(tc // 2, tc):
        x_copy(idxn_ref, i, 1 - slot).start(priority=i % 2)
    ye_ref[...] = jnp.dot(hid, wd_ref[0], preferred_element_type=f32).astype(bf16)

    @pl.when(s == last)
    def _():
        wait_tile(1 - slot)


def _ffn(idx, h2, wg, wu, wd):
    e, cap = idx.shape
    tc = _tile(cap, 256)
    nt = cap // tc
    steps = e * nt
    wspec = pl.BlockSpec((1, D_MODEL, D_FF), lambda si: (si // nt, 0, 0))
    idx3 = idx.reshape(steps, 1, tc)
    return pl.pallas_call(
        _ffn_kernel,
        grid=(steps,),
        in_specs=[
            pl.BlockSpec((1, 1, tc), lambda si: (si, 0, 0), memory_space=pltpu.SMEM),
            pl.BlockSpec((1, 1, tc), lambda si: (jnp.minimum(si + 1, steps - 1), 0, 0), memory_space=pltpu.SMEM),
            pl.BlockSpec(memory_space=pl.ANY),
            wspec, wspec,
            pl.BlockSpec((1, D_FF, D_MODEL), lambda si: (si // nt, 0, 0)),
        ],
        out_specs=pl.BlockSpec((tc, D_MODEL), lambda si: (si, 0)),
        out_shape=jax.ShapeDtypeStruct((e * cap, D_MODEL), bf16),
        scratch_shapes=[pltpu.VMEM((2, tc, D_MODEL // 2), jnp.uint32), pltpu.SemaphoreType.DMA((2,))],
        compiler_params=_cparams(("arbitrary",)),
        name="expert_ffn",
    )(idx3, idx3, h2, wg, wu, wd)


COMB_CH = 32
COMB_KB = 256
COMB_ALIGN = 2 * SUBLANES_V7X
META_START, META_NCH, META_OFF, META_SLOT, META_TOT = 0, 16, 32, 48, 64
META_W = 128


def _combine_kernel(meta_ref, metan_ref, x1_ref, pos_ref, aff_ref, ye_hbm, g_ref, o_ref, slab, oht, acc, sem):
    tn = x1_ref.shape[0]
    i = pl.program_id(0)
    last = pl.num_programs(0) - 1
    buf = i % 2

    def pieces(mref, b, fn):
        for e in range(N_EXPERTS):
            def body(c, carry, e=e):
                src = ye_hbm.at[pl.ds(pl.multiple_of(mref[0, 0, META_START + e] + c * COMB_CH, COMB_ALIGN),
                                      COMB_CH)]
                dst = slab.at[b, pl.ds(pl.multiple_of(mref[0, 0, META_OFF + e] + c * COMB_CH, COMB_CH), COMB_CH)]
                fn(pltpu.make_async_copy(src, dst, sem.at[b]))
                return carry
            lax.fori_loop(0, mref[0, 0, META_NCH + e], body, 0)

    @pl.when(i == 0)
    def _():
        slab[...] = jnp.zeros_like(slab)
        pieces(meta_ref, 0, lambda cp: cp.start())

    @pl.when(i < last)
    def _():
        pieces(metan_ref, 1 - buf, lambda cp: cp.start())

    rowi = lax.broadcasted_iota(i32, (COMB_CH, 1), 0)
    for e in range(N_EXPERTS):
        pos_e = pos_ref[e:e + 1, :]
        gate_e = aff_ref[e:e + 1, :]

        def fill(c, carry, e=e, pos_e=pos_e, gate_e=gate_e):
            slot = meta_ref[0, 0, META_SLOT + e] + c * COMB_CH + rowi
            row0 = pl.multiple_of(meta_ref[0, 0, META_OFF + e] + c * COMB_CH, COMB_CH)
            oht[pl.ds(row0, COMB_CH), :] = jnp.where(pos_e == slot, gate_e, 0.0).astype(bf16)
            return carry
        lax.fori_loop(0, meta_ref[0, 0, META_NCH + e], fill, 0)

    ktot = meta_ref[0, 0, META_TOT]
    nkb = (ktot + COMB_KB - 1) // COMB_KB

    def clear(c, carry):
        row0 = pl.multiple_of(ktot + c * COMB_CH, COMB_CH)
        oht[pl.ds(row0, COMB_CH), :] = jnp.zeros((COMB_CH, tn), bf16)
        return carry
    lax.fori_loop(0, (nkb * COMB_KB - ktot) // COMB_CH, clear, 0)

    pieces(meta_ref, buf, lambda cp: cp.wait())
    acc[...] = x1_ref[...]

    def mm(j, carry):
        rows = pl.ds(pl.multiple_of(j * COMB_KB, COMB_KB), COMB_KB)
        acc[...] += lax.dot_general(oht[rows, :], slab[buf, rows, :], TN, preferred_element_type=f32)
        return carry
    lax.fori_loop(0, nkb, mm, 0)
    o_ref[...] = _rms(acc[...], g_ref[...])


def _combine(meta, x1, pos, aff_t, ye, g):
    n = x1.shape[0]
    tn = _tile(n, 256)
    nt = n // tn
    kmax = N_EXPERTS * (tn + 2 * COMB_CH)
    assert kmax % COMB_KB == 0
    return pl.pallas_call(
        _combine_kernel,
        grid=(nt,),
        in_specs=[
            pl.BlockSpec((1, 1, META_W), lambda i: (i, 0, 0), memory_space=pltpu.SMEM),
            pl.BlockSpec((1, 1, META_W), lambda i: (jnp.minimum(i + 1, nt - 1), 0, 0), memory_space=pltpu.SMEM),
            pl.BlockSpec((tn, D_MODEL), lambda i: (i, 0)),
            pl.BlockSpec((N_EXPERTS, tn), lambda i: (0, i)),
            pl.BlockSpec((N_EXPERTS, tn), lambda i: (0, i)),
            pl.BlockSpec(memory_space=pl.ANY),
            pl.BlockSpec((1, D_MODEL), lambda i: (0, 0)),
        ],
        out_specs=pl.BlockSpec((tn, D_MODEL), lambda i: (i, 0)),
        out_shape=jax.ShapeDtypeStruct((n, D_MODEL), f32),
        scratch_shapes=[pltpu.VMEM((2, kmax, D_MODEL), bf16), pltpu.VMEM((kmax, tn), bf16),
                        pltpu.VMEM((tn, D_MODEL), f32), pltpu.SemaphoreType.DMA((2,))],
        compiler_params=_cparams(("arbitrary",)),
        name="combine_norm",
    )(meta, meta, x1, pos, aff_t, ye, g)


def _combine_meta(cum, cap, tn):
    e, n = cum.shape
    nt = n // tn
    end = cum[:, tn - 1::tn]
    before = jnp.concatenate([jnp.zeros((e, 1), i32), end[:, :-1]], axis=1)
    base = jnp.arange(e, dtype=i32)[:, None] * cap
    start = (base + before) // COMB_ALIGN * COMB_ALIGN
    nch = jnp.where(end > before, (base + end - start + COMB_CH - 1) // COMB_CH, 0)
    start = jnp.minimum(start, e * cap - nch * COMB_CH)
    rows = nch * COMB_CH
    off = jnp.cumsum(rows, axis=0) - rows
    tot = jnp.sum(rows, axis=0, keepdims=True)
    pad = jnp.zeros((META_W - META_TOT - 1, nt), i32)
    meta = jnp.concatenate([start, nch, off, start - base, tot, pad], axis=0)
    return meta.T.reshape(nt, 1, META_W)


def _prepare_params(norm_mix, w_in, shift_mu_prev, shift_mu_next, decay_w0, decay_up, iclr_a0, iclr_up,
                    gate_up, k_k, k_a, r_k, lnx_w, lnx_b, w_o_rwkv, dw_kernel, dw_bias, conv_ln_w, conv_ln_b,
                    w_o_conv, w_out, norm_ffn, router_w, router_b, exp_w_gate, exp_w_up, exp_w_down, norm_final):
    l = 0
    rwkv_cols = RKV_COLS + LORA_COLS
    w = w_in[l]
    pad = jnp.zeros((D_MODEL, LORA_PAD - LORA_COLS), f32)
    wa = jnp.concatenate([w[:, :rwkv_cols], pad], axis=1).astype(bf16)
    wu = w[:, rwkv_cols:rwkv_cols + 2 * C_CONV].astype(bf16)
    wgate = w[:, rwkv_cols + 2 * C_CONV:].astype(bf16)
    padv = jnp.zeros((LORA_PAD - LORA_COLS,), f32)
    mup = jnp.concatenate([shift_mu_prev[l], padv])[None, :]
    mun = jnp.concatenate([shift_mu_next[l], padv])[None, :]
    wup = jnp.zeros((LORA_PAD, UP_COLS), f32)
    wup = wup.at[0:W_LORA, 0:D_RWKV].set(decay_up[l, 0])
    wup = wup.at[W_LORA:2 * W_LORA, D_RWKV:2 * D_RWKV].set(decay_up[l, 1])
    o = 2 * W_LORA
    wup = wup.at[o:o + A_LORA, 2 * D_RWKV:3 * D_RWKV].set(iclr_up[l, 0])
    wup = wup.at[o + A_LORA:o + 2 * A_LORA, 3 * D_RWKV:4 * D_RWKV].set(iclr_up[l, 1])
    o = 2 * W_LORA + 2 * A_LORA
    wup = wup.at[o:o + G_LORA, 4 * D_RWKV:5 * D_RWKV].set(gate_up[l])
    head = jnp.arange(D_RWKV) // HEAD
    same = (head[:, None] == head[None, :])
    return dict(
        norm_mix=norm_mix[l][None, :], wa=wa, wu=wu, wgate=wgate, mup=mup, mun=mun, wup=wup.astype(bf16),
        w0=decay_w0[l], a0=iclr_a0[l], kkw=k_k[l][None, :], ka=k_a[l][None, :], rk=r_k[l].reshape(1, D_RWKV),
        hsum=same.astype(bf16), havg=(same.astype(f32) / HEAD).astype(bf16),
        lnxw=lnx_w[l][None, :], lnxb=lnx_b[l][None, :], wor=w_o_rwkv[l].astype(bf16),
        kern=dw_kernel[l], cbias=dw_bias[l][None, :], clnw=conv_ln_w[l][None, :], clnb=conv_ln_b[l][None, :],
        woc=w_o_conv[l].astype(bf16), wout=w_out[l].astype(bf16), nffn=norm_ffn[l][None, :],
        rwt=router_w[l].T.astype(bf16), rb=router_b[l][:, None],
        ewg=exp_w_gate[l].astype(bf16), ewu=exp_w_up[l].astype(bf16), ewd=exp_w_down[l].astype(bf16),
        nfinal=norm_final[None, :],
    )


def _trunk(x, P):
    b, t, d = x.shape
    n = b * t
    x2 = x.reshape(n, d)
    p, u = _inproj(x2, P["norm_mix"], P["wa"], P["wu"])
    r, v, kk, ew0, ew1, k0, k1, b0, b1, g, bonus = _prep(
        p.reshape(b, t, PA_COLS), P["mup"], P["mun"], P["wup"], P["w0"], P["a0"], P["kkw"], P["ka"], P["rk"],
        P["hsum"])
    yf, yb = _scan(r, v, kk, ew0, ew1, k0, k1, b0, b1)
    flat = lambda a: a.reshape(n, a.shape[-1])
    x1, h2, aff_t = _merge(t, x2, flat(yf), flat(yb), flat(g), flat(bonus), u, P["kern"], P["cbias"],
                           P["clnw"], P["clnb"], P["norm_mix"], P["wgate"], P["lnxw"], P["lnxb"], P["havg"],
                           P["wor"], P["woc"], P["wout"], P["nffn"], P["rwt"], P["rb"])
    cap = max(1, CAPACITY_FACTOR * n // N_EXPERTS)
    idx, cum, pos = _select(aff_t, cap)
    ye = _ffn(idx, h2, P["ewg"], P["ewu"], P["ewd"])
    meta = _combine_meta(cum, cap, _tile(n, 256))
    return _combine(meta, x1, pos, aff_t, ye, P["nfinal"]).reshape(b, t, d)


def kernel(x_prompt, x_sample, norm_mix, w_in, shift_mu_prev, shift_mu_next, decay_w0, decay_up, iclr_a0, iclr_up, gate_up, k_k, k_a, r_k, lnx_w, lnx_b, w_o_rwkv, dw_kernel, dw_bias, conv_ln_w, conv_ln_b, w_o_conv, w_out, norm_ffn, router_w, router_b, exp_w_gate, exp_w_up, exp_w_down, norm_final):
    P = _prepare_params(norm_mix, w_in, shift_mu_prev, shift_mu_next, decay_w0, decay_up, iclr_a0, iclr_up,
                        gate_up, k_k, k_a, r_k, lnx_w, lnx_b, w_o_rwkv, dw_kernel, dw_bias, conv_ln_w,
                        conv_ln_b, w_o_conv, w_out, norm_ffn, router_w, router_b, exp_w_gate, exp_w_up,
                        exp_w_down, norm_final)
    return (_trunk(x_prompt, P), _trunk(x_sample, P))
```
